```python
import math, functools
import jax, jax.numpy as jnp
from jax import lax
import numpy as np

D_MODEL = 1024
BATCH = 4
SEQ = 4096
DEPTH = 1
DEC_BATCH = 128
DEC_SEQ = 1
PAST_LEN = 8192
PAGE_SIZE = 128

D_RNN = D_MODEL
N_RNN_BLOCKS = 8
RNN_BLOCK = D_RNN // N_RNN_BLOCKS
CONV_WIDTH = 4
LRU_C = 8.0
N_HEADS = 8
N_KV_HEADS = 4
N_REP = N_HEADS // N_KV_HEADS
HEAD_DIM = D_MODEL // N_HEADS // 2
V_DIM = 2 * HEAD_DIM
ROPE_THETA = 10000.0
Q_BLOCK = 128
N_EXPERTS = 32
TOP_K = 4
D_FF = D_MODEL
SWIGLU_ALPHA = 1.702
SWIGLU_LIMIT = 7.0
EPS = 1e-6
Q_W = N_HEADS * 2 * HEAD_DIM
K_W = N_KV_HEADS * 2 * HEAD_DIM
V_W = N_KV_HEADS * V_DIM
IN_SPLITS = (D_RNN, 2 * D_RNN, 2 * D_RNN + Q_W, 2 * D_RNN + Q_W + K_W,
             2 * D_RNN + Q_W + K_W + V_W, 2 * D_RNN + Q_W + K_W + V_W + D_MODEL)
D_IN = IN_SPLITS[-1] + D_MODEL
F32 = jnp.float32

kernel_name = 'griffin_diffattn_moe_decoder_step'


def _rms_norm(x, g):
    xf = x.astype(F32)
    y = xf * lax.rsqrt(jnp.mean(xf * xf, axis=-1, keepdims=True) + EPS)
    return (y * g.astype(F32)).astype(x.dtype)


def _rope(x, pos):
    half = HEAD_DIM // 2
    inv = ROPE_THETA ** (-jnp.arange(half, dtype=F32) / half)
    ang = pos.astype(F32)[:, None] * inv[None, :]
    shp = (1, pos.shape[0]) + (1,) * (x.ndim - 3) + (half,)
    cos = jnp.cos(ang).reshape(shp)
    sin = jnp.sin(ang).reshape(shp)
    xf = x.astype(F32)
    x1, x2 = xf[..., :half], xf[..., half:]
    return jnp.concatenate([x1 * cos - x2 * sin, x2 * cos + x1 * sin], axis=-1).astype(x.dtype)


def _causal_conv(x, conv_state, w, b):
    xp = jnp.concatenate([conv_state.astype(x.dtype), x], axis=1)
    T = x.shape[1]
    y = b
    for j in range(CONV_WIDTH):
        y = y + xp[:, j:j + T] * w[j]
    return y, xp[:, xp.shape[1] - (CONV_WIDTH - 1):]


def _block_diag(x, w, b):
    xb = x.reshape(x.shape[:-1] + (N_RNN_BLOCKS, RNN_BLOCK))
    y = jnp.einsum('btnc,ncd->btnd', xb, w) + b
    return y.reshape(x.shape)


def _lru_combine(left, right):
    a_l, b_l = left
    a_r, b_r = right
    return a_l * a_r, a_r * b_l + b_r


def _rg_lru(x, pos, h0, w_a, b_a, w_x, b_x, lam):
    r = jax.nn.sigmoid(_block_diag(x, w_a, b_a).astype(F32))
    i = jax.nn.sigmoid(_block_diag(x, w_x, b_x).astype(F32))
    log_a = -LRU_C * r * jax.nn.softplus(-lam.astype(F32))
    a = jnp.exp(log_a)
    mult = jnp.sqrt(-jnp.expm1(2.0 * log_a))
    reset = (pos == 0)[None, :, None]
    a = jnp.where(reset, 0.0, a)
    mult = jnp.where(reset, 1.0, mult)
    bx = mult * i * x.astype(F32)
    a_cum, b_cum = lax.associative_scan(_lru_combine, (a, bx), axis=1)
    h = a_cum * h0.astype(F32)[:, None, :] + b_cum
    return h.astype(x.dtype), h[:, -1]


def _diff_partial(q, k, v, mask):
    s = jnp.einsum('btkgmd,bskmd->bkgmts', q, k).astype(F32)
    if mask is not None:
        s = jnp.where(mask, s, -jnp.inf)
    m = jnp.max(s, axis=-1)
    pr = jnp.exp(s - m[..., None])
    l = jnp.sum(pr, axis=-1)
    acc = jnp.einsum('bkgmts,bskv->bkgmtv', pr, v.astype(F32))
    return m, l, acc


def _merge(a, b):
    m1, l1, o1 = a
    m2, l2, o2 = b
    m = jnp.maximum(m1, m2)
    c1 = jnp.exp(m1 - m)
    c2 = jnp.exp(m2 - m)
    return m, c1 * l1 + c2 * l2, c1[..., None] * o1 + c2[..., None] * o2


def _diff_finalize(part, lam, g_subln, lam_init):
    _, l, acc = part
    o = acc / l[..., None]
    o = o[:, :, :, 0] - lam * o[:, :, :, 1]
    o = jnp.transpose(o, (0, 3, 1, 2, 4))
    o = _rms_norm(o, g_subln) * (1.0 - lam_init)
    return o.reshape(o.shape[0], o.shape[1], N_HEADS * V_DIM)


def _prompt_attention(q, k, v, lam, g_subln, lam_init):
    B, S = q.shape[:2]
    nb = S // Q_BLOCK
    qb = jnp.moveaxis(q.reshape((B, nb, Q_BLOCK) + q.shape[2:]), 1, 0)
    kpos = jnp.arange(S)

    def block(args):
        qi, bi = args
        qpos = bi * Q_BLOCK + jnp.arange(Q_BLOCK)
        mask = qpos[:, None] >= kpos[None, :]
        return _diff_finalize(_diff_partial(qi, k, v, mask), lam, g_subln, lam_init)

    out = lax.map(block, (qb, jnp.arange(nb)))
    return jnp.moveaxis(out, 0, 1).reshape(B, S, N_HEADS * V_DIM)


def _sample_attention(q, k, v, lam, g_subln, lam_init, cache_k, cache_v, layer, page_table):
    DB, T = q.shape[:2]
    ps = cache_k.shape[2]
    stat = (DB, N_KV_HEADS, N_REP, 2, T)
    init = (jnp.full(stat, -jnp.inf, F32), jnp.zeros(stat, F32), jnp.zeros(stat + (V_DIM,), F32))

    def page_step(carry, pcol):
        kp = cache_k[layer, pcol].reshape(DB, ps, N_KV_HEADS, 2, HEAD_DIM)
        vp = cache_v[layer, pcol]
        return _merge(carry, _diff_partial(q, kp, vp, None)), None

    part, _ = lax.scan(page_step, init, page_table.T)
    self_mask = jnp.arange(T)[:, None] >= jnp.arange(T)[None, :]
    part = _merge(part, _diff_partial(q, k, v, self_mask))
    return _diff_finalize(part, lam, g_subln, lam_init)


def _token_mixer(h, pos, p, lam_init, attend, conv_state, h0):
    B, T, _ = h.shape
    u = h @ p['w_in']
    xr, yr, q, k, v, g_r, g_a = jnp.split(u, IN_SPLITS, axis=-1)
    xr, conv_new = _causal_conv(xr, conv_state, p['conv_w'], p['conv_b'])
    hr, h_new = _rg_lru(xr, pos, h0, p['w_gate_a'], p['b_gate_a'], p['w_gate_x'], p['b_gate_x'], p['lru_lambda'])
    rnn_out = (hr * jax.nn.gelu(yr)) @ p['w_rnn_out']
    q = _rope(q.reshape(B, T, N_HEADS, 2, HEAD_DIM), pos) * (HEAD_DIM ** -0.5)
    q = q.reshape(B, T, N_KV_HEADS, N_REP, 2, HEAD_DIM)
    k = _rope(k.reshape(B, T, N_KV_HEADS, 2, HEAD_DIM), pos)
    v = v.reshape(B, T, N_KV_HEADS, V_DIM)
    lam = (jnp.exp(jnp.sum(p['lambda_q1'].astype(F32) * p['lambda_k1'].astype(F32)))
           - jnp.exp(jnp.sum(p['lambda_q2'].astype(F32) * p['lambda_k2'].astype(F32))) + lam_init)
    attn = attend(q, k, v, lam, p['g_subln'], lam_init).astype(h.dtype) @ p['w_attn_out']
    merged = jax.nn.sigmoid(g_r) * rnn_out + jax.nn.sigmoid(g_a) * attn
    out = merged @ p['w_mix_out']
    return out, (k.reshape(B, T, N_KV_HEADS, 2 * HEAD_DIM), v, conv_new, h_new)


def _moe(h, p):
    B, T, D = h.shape
    xt = h.reshape(B * T, D)
    logits = (xt @ p['w_router'] + p['b_router']).astype(F32)
    top_v, top_i = lax.top_k(logits, TOP_K)
    gates = jnp.sum(jax.nn.one_hot(top_i, N_EXPERTS, dtype=F32) * jax.nn.softmax(top_v, axis=-1)[..., None], axis=1)
    y = jnp.zeros((B * T, D), F32)
    for e in range(N_EXPERTS):
        glu = jnp.minimum(xt @ p['w_gate_e'][e] + p['b_gate_e'][e], SWIGLU_LIMIT)
        lin = jnp.clip(xt @ p['w_up_e'][e] + p['b_up_e'][e], -SWIGLU_LIMIT, SWIGLU_LIMIT)
        hid = glu * jax.nn.sigmoid(SWIGLU_ALPHA * glu) * (lin + 1.0)
        y = y + gates[:, e:e + 1] * (hid @ p['w_down_e'][e] + p['b_down_e'][e])
    return y.reshape(B, T, D).astype(h.dtype)


def _layer(x, c, pos, p, lam_init, attend, conv_state, h0):
    mod = jax.nn.silu(c) @ p['w_ada'] + p['b_ada']
    sh1, sc1, g1, sh2, sc2, g2 = [m[:, None, :] for m in jnp.split(mod, 6, axis=-1)]
    h = _rms_norm(x, p['g_pre_mix']) * (1.0 + sc1) + sh1
    mix, state = _token_mixer(h, pos, p, lam_init, attend, conv_state, h0)
    x = x + g1 * _rms_norm(mix, p['g_post_mix'])
    h = _rms_norm(x, p['g_pre_ffn']) * (1.0 + sc2) + sh2
    x = x + g2 * _rms_norm(_moe(h, p), p['g_post_ffn'])
    return x, state


def setup_inputs(seed: int = 0) -> dict:
    key = jax.random.key(seed)
    keys = jax.random.split(key, 64)
    ctr = iter(range(64))

    def nrm(shape, scale):
        return jax.random.normal(keys[next(ctr)], shape, F32) * scale

    def gain(n):
        return 1.0 + nrm((DEPTH, n), 0.05)

    n_pages = PAST_LEN // PAGE_SIZE
    n_used = DEC_BATCH * n_pages
    n_pool = n_used + n_used // 4
    page_table = jax.random.permutation(keys[next(ctr)], n_pool)[:n_used].reshape(DEC_BATCH, n_pages).astype(jnp.int32)
    u = jax.random.uniform(keys[next(ctr)], (DEPTH, D_RNN), F32, 0.9, 0.999)
    lru_lambda = jnp.log(u) - jnp.log1p(-u)
    inp = {}
    inp['x_prompt'] = nrm((BATCH, SEQ, D_MODEL), 1.0)
    inp['x_sample'] = nrm((DEC_BATCH, DEC_SEQ, D_MODEL), 1.0)
    inp['cache_k'] = nrm((DEPTH, n_pool, PAGE_SIZE, N_KV_HEADS, 2 * HEAD_DIM), 1.0)
    inp['cache_v'] = nrm((DEPTH, n_pool, PAGE_SIZE, N_KV_HEADS, V_DIM), 1.0)
    inp['state_conv'] = nrm((DEPTH, DEC_BATCH, CONV_WIDTH - 1, D_RNN), 1.0)
    inp['state_rglru'] = nrm((DEPTH, DEC_BATCH, D_RNN), 0.5)
    inp['page_table'] = page_table
    inp['c_prompt'] = nrm((BATCH, D_MODEL), 1.0)
    inp['c_sample'] = nrm((DEC_BATCH, D_MODEL), 1.0)
    inp['w_ada'] = nrm((DEPTH, D_MODEL, 6 * D_MODEL), 0.5 * D_MODEL ** -0.5)
    inp['b_ada'] = nrm((DEPTH, 6 * D_MODEL), 0.01)
    inp['g_pre_mix'] = gain(D_MODEL)
    inp['g_post_mix'] = gain(D_MODEL)
    inp['g_pre_ffn'] = gain(D_MODEL)
    inp['g_post_ffn'] = gain(D_MODEL)
    inp['w_in'] = nrm((DEPTH, D_MODEL, D_IN), D_MODEL ** -0.5)
    inp['conv_w'] = nrm((DEPTH, CONV_WIDTH, D_RNN), CONV_WIDTH ** -0.5)
    inp['conv_b'] = nrm((DEPTH, D_RNN), 0.01)
    inp['w_gate_a'] = nrm((DEPTH, N_RNN_BLOCKS, RNN_BLOCK, RNN_BLOCK), RNN_BLOCK ** -0.5)
    inp['b_gate_a'] = nrm((DEPTH, N_RNN_BLOCKS, RNN_BLOCK), 0.01)
    inp['w_gate_x'] = nrm((DEPTH, N_RNN_BLOCKS, RNN_BLOCK, RNN_BLOCK), RNN_BLOCK ** -0.5)
    inp['b_gate_x'] = nrm((DEPTH, N_RNN_BLOCKS, RNN_BLOCK), 0.01)
    inp['lru_lambda'] = lru_lambda
    inp['w_rnn_out'] = nrm((DEPTH, D_RNN, D_MODEL), D_RNN ** -0.5)
    inp['lambda_q1'] = nrm((DEPTH, HEAD_DIM), 0.1)
    inp['lambda_k1'] = nrm((DEPTH, HEAD_DIM), 0.1)
    inp['lambda_q2'] = nrm((DEPTH, HEAD_DIM), 0.1)
    inp['lambda_k2'] = nrm((DEPTH, HEAD_DIM), 0.1)
    inp['g_subln'] = gain(V_DIM)
    inp['w_attn_out'] = nrm((DEPTH, N_HEADS * V_DIM, D_MODEL), (N_HEADS * V_DIM) ** -0.5)
    inp['w_mix_out'] = nrm((DEPTH, D_MODEL, D_MODEL), D_MODEL ** -0.5)
    inp['w_router'] = nrm((DEPTH, D_MODEL, N_EXPERTS), D_MODEL ** -0.5)
    inp['b_router'] = nrm((DEPTH, N_EXPERTS), 0.01)
    inp['w_gate_e'] = nrm((DEPTH, N_EXPERTS, D_MODEL, D_FF), D_MODEL ** -0.5)
    inp['b_gate_e'] = nrm((DEPTH, N_EXPERTS, D_FF), 0.01)
    inp['w_up_e'] = nrm((DEPTH, N_EXPERTS, D_MODEL, D_FF), D_MODEL ** -0.5)
    inp['b_up_e'] = nrm((DEPTH, N_EXPERTS, D_FF), 0.01)
    inp['w_down_e'] = nrm((DEPTH, N_EXPERTS, D_FF, D_MODEL), D_FF ** -0.5)
    inp['b_down_e'] = nrm((DEPTH, N_EXPERTS, D_MODEL), 0.01)
    return inp


def reference(x_prompt, x_sample, cache_k, cache_v, state_conv, state_rglru, page_table, c_prompt, c_sample,
              w_ada, b_ada, g_pre_mix, g_post_mix, g_pre_ffn, g_post_ffn, w_in, conv_w, conv_b,
              w_gate_a, b_gate_a, w_gate_x, b_gate_x, lru_lambda, w_rnn_out,
              lambda_q1, lambda_k1, lambda_q2, lambda_k2, g_subln, w_attn_out, w_mix_out,
              w_router, b_router, w_gate_e, b_gate_e, w_up_e, b_up_e, w_down_e, b_down_e):
    S = x_prompt.shape[1]
    T = x_sample.shape[1]
    past = page_table.shape[1] * cache_k.shape[2]
    pos_p = jnp.arange(S, dtype=jnp.int32)
    pos_s = past + jnp.arange(T, dtype=jnp.int32)
    xp, xs = x_prompt, x_sample
    kp_l, vp_l, ks_l, vs_l, cp_l, cs_l, hp_l, hs_l = [], [], [], [], [], [], [], []
    for l in range(DEPTH):
        p = dict(w_ada=w_ada[l], b_ada=b_ada[l], g_pre_mix=g_pre_mix[l], g_post_mix=g_post_mix[l],
                 g_pre_ffn=g_pre_ffn[l], g_post_ffn=g_post_ffn[l], w_in=w_in[l], conv_w=conv_w[l], conv_b=conv_b[l],
                 w_gate_a=w_gate_a[l], b_gate_a=b_gate_a[l], w_gate_x=w_gate_x[l], b_gate_x=b_gate_x[l],
                 lru_lambda=lru_lambda[l], w_rnn_out=w_rnn_out[l], lambda_q1=lambda_q1[l], lambda_k1=lambda_k1[l],
                 lambda_q2=lambda_q2[l], lambda_k2=lambda_k2[l], g_subln=g_subln[l], w_attn_out=w_attn_out[l],
                 w_mix_out=w_mix_out[l], w_router=w_router[l], b_router=b_router[l], w_gate_e=w_gate_e[l],
                 b_gate_e=b_gate_e[l], w_up_e=w_up_e[l], b_up_e=b_up_e[l], w_down_e=w_down_e[l], b_down_e=b_down_e[l])
        lam_init = 0.8 - 0.6 * math.exp(-0.3 * l)
        zero_conv = jnp.zeros((xp.shape[0], CONV_WIDTH - 1, D_RNN), xp.dtype)
        zero_h = jnp.zeros((xp.shape[0], D_RNN), F32)
        xp, (kp, vp, cp, hp) = _layer(xp, c_prompt, pos_p, p, lam_init, _prompt_attention, zero_conv, zero_h)
        attend_s = functools.partial(_sample_attention, cache_k=cache_k, cache_v=cache_v, layer=l, page_table=page_table)
        xs, (ks, vs, cs, hs) = _layer(xs, c_sample, pos_s, p, lam_init, attend_s, state_conv[l], state_rglru[l])
        kp_l.append(kp); vp_l.append(vp); ks_l.append(ks); vs_l.append(vs)
        cp_l.append(cp); cs_l.append(cs); hp_l.append(hp); hs_l.append(hs)
    return (xp, xs, jnp.stack(kp_l), jnp.stack(vp_l), jnp.stack(ks_l), jnp.stack(vs_l),
            jnp.stack(cp_l), jnp.stack(cs_l), jnp.stack(hp_l), jnp.stack(hs_l))
```

```python
import functools
import math

import jax
import jax.numpy as jnp
from jax import lax
from jax.experimental import pallas as pl
from jax.experimental.pallas import tpu as pltpu

F32 = jnp.float32
BF16 = jnp.bfloat16

N_HEADS = 8
N_KV_HEADS = 4
HEAD_DIM = 64
V_DIM = 2 * HEAD_DIM
LRU_C = 8.0
ROPE_THETA = 10000.0
TOP_K = 4
SWIGLU_ALPHA = 1.702
SWIGLU_LIMIT = 7.0
EPS = 1e-6

LANES = 128
NEW_ROWS = 16
VMEM_LIMIT = 56 * 1024 * 1024
NEG_INF = float("-inf")


def _cparams(*sem):
    return pltpu.CompilerParams(dimension_semantics=sem, vmem_limit_bytes=VMEM_LIMIT)


def _rms(x, g):
    return x * lax.rsqrt(jnp.mean(x * x, axis=-1, keepdims=True) + EPS) * g


def _ada_kernel(c_ref, w_ref, b_ref, o_ref):
    c = c_ref[...]
    s = (c * jax.nn.sigmoid(c)).astype(BF16)
    o_ref[...] = jnp.dot(s, w_ref[...].astype(BF16), preferred_element_type=F32) + b_ref[...]


def _ada(c, w, b):
    m, d = c.shape
    n = w.shape[1]
    tn = 1024
    return pl.pallas_call(
        _ada_kernel,
        grid=(n // tn,),
        in_specs=[pl.BlockSpec((m, d), lambda j: (0, 0)),
                  pl.BlockSpec((d, tn), lambda j: (0, j)),
                  pl.BlockSpec((1, tn), lambda j: (0, j))],
        out_specs=pl.BlockSpec((m, tn), lambda j: (0, j)),
        out_shape=jax.ShapeDtypeStruct((m, n), F32),
        compiler_params=_cparams("arbitrary"),
        name="ada_mod",
    )(c, w, b.reshape(1, n))


def _inproj_kernel(x_ref, sc_ref, sh_ref, g_ref, w_ref, o_ref):
    h = _rms(x_ref[0], g_ref[...])
    h = h * (1.0 + sc_ref[0]) + sh_ref[0]
    o_ref[0] = jnp.dot(h.astype(BF16), w_ref[...], preferred_element_type=F32)


def _mod_spec(mod, tm, n_lead):
    per_row = mod.shape[1] != 1
    d = mod.shape[2]
    rows = tm if per_row else 1
    if n_lead == 1:
        return pl.BlockSpec((1, rows, d), lambda n, g, i: (g, i if per_row else 0, 0))
    return pl.BlockSpec((1, rows, d), lambda g, i: (g, i if per_row else 0, 0))


def _inproj(x, sc, sh, gain, w_bf, tm):
    g_, t, d = x.shape
    n = w_bf.shape[1]
    tn = 2048
    return pl.pallas_call(
        _inproj_kernel,
        grid=(n // tn, g_, t // tm),
        in_specs=[pl.BlockSpec((1, tm, d), lambda n, g, i: (g, i, 0)),
                  _mod_spec(sc, tm, 1), _mod_spec(sh, tm, 1),
                  pl.BlockSpec((1, d), lambda n, g, i: (0, 0)),
                  pl.BlockSpec((d, tn), lambda n, g, i: (0, n))],
        out_specs=pl.BlockSpec((1, tm, tn), lambda n, g, i: (g, i, n)),
        out_shape=jax.ShapeDtypeStruct((g_, t, n), F32),
        compiler_params=_cparams("arbitrary", "arbitrary", "arbitrary"),
        name="in_proj",
    )(x, sc, sh, gain.reshape(1, d), w_bf)


def _rope_heads(x, c, s, first_half):
    outs = []
    for h in range(x.shape[1] // LANES):
        xh = x[:, h * LANES:(h + 1) * LANES]
        swapped = jnp.where(first_half, pltpu.roll(xh, LANES - HEAD_DIM // 2, 1),
                            pltpu.roll(xh, HEAD_DIM // 2, 1))
        outs.append(xh * c + swapped * s)
    return jnp.concatenate(outs, axis=1)


def _prep_kernel(q_ref, k_ref, v_ref, c_ref, s_ref, qb_ref, kf_ref, kb_ref, vf_ref, vb_ref):
    c = c_ref[...]
    s = s_ref[...]
    lane = lax.broadcasted_iota(jnp.int32, c.shape, 1)
    first_half = (lane & (HEAD_DIM - 1)) < HEAD_DIM // 2
    q = _rope_heads(q_ref[0], c, s, first_half) * (HEAD_DIM ** -0.5)
    k = _rope_heads(k_ref[0], c, s, first_half)
    v = v_ref[0]
    qb_ref[0] = q.astype(BF16)
    kf_ref[0] = k
    kb_ref[0] = k.astype(BF16)
    vf_ref[0] = v
    vb_ref[0] = v.astype(BF16)


def _prep(u, cos_t, sin_t, tm, q_w, k_w, v_w, q_off, k_off, v_off):
    g_, t, _ = u.shape
    return pl.pallas_call(
        _prep_kernel,
        grid=(g_, t // tm),
        in_specs=[pl.BlockSpec((1, tm, q_w), lambda g, i: (g, i, q_off // q_w)),
                  pl.BlockSpec((1, tm, k_w), lambda g, i: (g, i, k_off // k_w)),
                  pl.BlockSpec((1, tm, v_w), lambda g, i: (g, i, v_off // v_w)),
                  pl.BlockSpec((tm, LANES), lambda g, i: (i, 0)),
                  pl.BlockSpec((tm, LANES), lambda g, i: (i, 0))],
        out_specs=[pl.BlockSpec((1, tm, q_w), lambda g, i: (g, i, 0)),
                   pl.BlockSpec((1, tm, k_w), lambda g, i: (g, i, 0)),
                   pl.BlockSpec((1, tm, k_w), lambda g, i: (g, i, 0)),
                   pl.BlockSpec((1, tm, v_w), lambda g, i: (g, i, 0)),
                   pl.BlockSpec((1, tm, v_w), lambda g, i: (g, i, 0))],
        out_shape=[jax.ShapeDtypeStruct((g_, t, q_w), BF16),
                   jax.ShapeDtypeStruct((g_, t, k_w), F32),
                   jax.ShapeDtypeStruct((g_, t, k_w), BF16),
                   jax.ShapeDtypeStruct((g_, t, v_w), F32),
                   jax.ShapeDtypeStruct((g_, t, v_w), BF16)],
        compiler_params=_cparams("arbitrary", "arbitrary"),
        name="qkv_prep",
    )(u, u, u, cos_t, sin_t)


def _flash_kernel(lam_ref, q_ref, k_ref, v_ref, gs_ref, o_ref, q4_scr, m_scr, l_scr, acc_scr,
                  *, tq, lam_init):
    i = pl.program_id(2)
    n_rep = N_HEADS // N_KV_HEADS
    q = q_ref[0]
    lane = lax.broadcasted_iota(jnp.int32, (tq, LANES), 1)
    is_map1 = lane < HEAD_DIM
    zero = jnp.zeros((tq, LANES), BF16)
    for g in range(n_rep):
        qg = q[:, g * LANES:(g + 1) * LANES]
        q4_scr[(2 * g) * tq:(2 * g + 1) * tq, :] = jnp.where(is_map1, qg, zero)
        q4_scr[(2 * g + 1) * tq:(2 * g + 2) * tq, :] = jnp.where(is_map1, zero, qg)
    m_scr[...] = jnp.full(m_scr.shape, NEG_INF, F32)
    l_scr[...] = jnp.zeros(l_scr.shape, F32)
    acc_scr[...] = jnp.zeros(acc_scr.shape, F32)
    rows = 2 * n_rep * tq

    def step(t, masked):
        start = pl.multiple_of(t * tq, tq)
        k = k_ref[0, pl.ds(start, tq), :]
        v = v_ref[0, pl.ds(start, tq), :]
        s = lax.dot_general(q4_scr[...], k, (((1,), (1,)), ((), ())), preferred_element_type=F32)
        if masked:
            r = lax.broadcasted_iota(jnp.int32, (rows, tq), 0) & (tq - 1)
            c = lax.broadcasted_iota(jnp.int32, (rows, tq), 1)
            s = jnp.where(r >= c, s, NEG_INF)
        m_prev = m_scr[...]
        m_new = jnp.maximum(m_prev, jnp.max(s, axis=1, keepdims=True))
        alpha = jnp.exp(m_prev - m_new)
        p = jnp.exp(s - jnp.concatenate([m_new] * (tq // LANES), axis=1))
        l_scr[...] = alpha * l_scr[...] + jnp.sum(p, axis=1, keepdims=True)
        acc_scr[...] = alpha * acc_scr[...] + jnp.dot(p.astype(BF16), v, preferred_element_type=F32)
        m_scr[...] = m_new

    def body(t, carry):
        step(t, False)
        return carry

    lax.fori_loop(0, i, body, 0)
    step(i, True)

    o = acc_scr[...] / l_scr[...]
    lam = lam_ref[0]
    for g in range(n_rep):
        o1 = o[(2 * g) * tq:(2 * g + 1) * tq, :]
        o2 = o[(2 * g + 1) * tq:(2 * g + 2) * tq, :]
        d = _rms(o1 - lam * o2, gs_ref[...]) * (1.0 - lam_init)
        o_ref[0, :, g * LANES:(g + 1) * LANES] = d.astype(BF16)


def _flash(lam, q_bf, k_bf, v_bf, g_subln, lam_init, tq):
    b, s, _ = q_bf.shape
    n_rep = N_HEADS // N_KV_HEADS
    qw = n_rep * 2 * HEAD_DIM
    rows = 2 * n_rep * tq
    kern = functools.partial(_flash_kernel, tq=tq, lam_init=lam_init)
    return pl.pallas_call(
        kern,
        grid=(b, N_KV_HEADS, s // tq),
        in_specs=[pl.BlockSpec(memory_space=pltpu.SMEM),
                  pl.BlockSpec((1, tq, qw), lambda b_, j, i: (b_, i, j)),
                  pl.BlockSpec((1, s, 2 * HEAD_DIM), lambda b_, j, i: (b_, 0, j)),
                  pl.BlockSpec((1, s, V_DIM), lambda b_, j, i: (b_, 0, j)),
                  pl.BlockSpec((1, V_DIM), lambda b_, j, i: (0, 0))],
        out_specs=pl.BlockSpec((1, tq, n_rep * V_DIM), lambda b_, j, i: (b_, i, j)),
        out_shape=jax.ShapeDtypeStruct((b, s, N_HEADS * V_DIM), BF16),
        scratch_shapes=[pltpu.VMEM((rows, LANES), BF16),
                        pltpu.VMEM((rows, LANES), F32),
                        pltpu.VMEM((rows, LANES), F32),
                        pltpu.VMEM((rows, V_DIM), F32)],
        compiler_params=_cparams("arbitrary", "arbitrary", "arbitrary"),
        name="prompt_attn",
    )(lam, q_bf, k_bf, v_bf, g_subln.reshape(1, V_DIM))


def _decode_kernel(pt_ref, lam_ref, qm_ref, kn_ref, vn_ref, gs_ref, *rest, pps, lam_init):
    k_refs = rest[:pps]
    v_refs = rest[pps:2 * pps]
    o_ref = rest[2 * pps]
    m_scr, l_scr, acc_scr = rest[2 * pps + 1:]
    s_id = pl.program_id(1)
    n_steps = pl.num_programs(1)

    @pl.when(s_id == 0)
    def _():
        m_scr[...] = jnp.full(m_scr.shape, NEG_INF, F32)
        l_scr[...] = jnp.zeros(l_scr.shape, F32)
        acc_scr[...] = jnp.zeros(acc_scr.shape, F32)

    qm = qm_ref[0]

    def update(scores, values):
        mx = m_scr[0:1, :]
        m_new = mx
        for s in scores:
            m_new = jnp.maximum(m_new, jnp.max(s, axis=0, keepdims=True))
        alpha = jnp.exp(mx - m_new)
        l_new = alpha * l_scr[0:1, :]
        acc = alpha * acc_scr[...]
        for s, v in zip(scores, values):
            p = jnp.exp(s - m_new)
            l_new = l_new + jnp.sum(p, axis=0, keepdims=True)
            acc = acc + lax.dot_general(v, p.astype(BF16), (((0,), (0,)), ((), ())),
                                        preferred_element_type=F32)
        m_scr[0:1, :] = m_new
        l_scr[0:1, :] = l_new
        acc_scr[...] = acc

    scores = []
    values = []
    for kr, vr in zip(k_refs, v_refs):
        kb = kr[0].astype(BF16)
        scores.append(jnp.dot(kb, qm, preferred_element_type=F32))
        values.append(vr[0].astype(BF16))
    update(scores, values)

    @pl.when(s_id == n_steps - 1)
    def _():
        kn = kn_ref[0]
        s_self = jnp.dot(kn, qm, preferred_element_type=F32)
        row = lax.broadcasted_iota(jnp.int32, s_self.shape, 0)
        s_self = jnp.where(row == 0, s_self, NEG_INF)
        update([s_self], [vn_ref[0]])
        o = acc_scr[...] / l_scr[0:1, :]
        ot = o.T
        lam = lam_ref[0]
        d = ot[0:N_HEADS, :] - lam * ot[N_HEADS:2 * N_HEADS, :]
        head = lax.broadcasted_iota(jnp.int32, (N_HEADS, V_DIM), 0)
        kv_of_head = head // (N_HEADS // N_KV_HEADS)
        sel = jnp.zeros((N_HEADS, V_DIM), F32)
        for j in range(N_KV_HEADS):
            sel = sel + jnp.where(kv_of_head == j, d[:, j * V_DIM:(j + 1) * V_DIM], 0.0)
        o_ref[0] = _rms(sel, gs_ref[...]) * (1.0 - lam_init)


def _decode(page_table, lam, qmat, k_new, v_new, g_subln, cache_k, cache_v, lam_init, pps):
    db, n_pages = page_table.shape
    n_pool, page, kw = cache_k.shape
    vw = cache_v.shape[2]
    pt_flat = page_table.reshape(-1)

    def page_spec(width, i):
        return pl.BlockSpec((1, page, width),
                            lambda b, s, pt: (pt[b * n_pages + s * pps + i], 0, 0))

    kern = functools.partial(_decode_kernel, pps=pps, lam_init=lam_init)
    grid_spec = pltpu.PrefetchScalarGridSpec(
        num_scalar_prefetch=1,
        grid=(db, n_pages // pps),
        in_specs=[pl.BlockSpec(memory_space=pltpu.SMEM),
                  pl.BlockSpec((1, kw, LANES), lambda b, s, pt: (b, 0, 0)),
                  pl.BlockSpec((1, NEW_ROWS, kw), lambda b, s, pt: (b, 0, 0)),
                  pl.BlockSpec((1, NEW_ROWS, vw), lambda b, s, pt: (b, 0, 0)),
                  pl.BlockSpec((1, V_DIM), lambda b, s, pt: (0, 0))]
                 + [page_spec(kw, i) for i in range(pps)]
                 + [page_spec(vw, i) for i in range(pps)],
        out_specs=pl.BlockSpec((1, N_HEADS, V_DIM), lambda b, s, pt: (b, 0, 0)),
        scratch_shapes=[pltpu.VMEM((8, LANES), F32),
                        pltpu.VMEM((8, LANES), F32),
                        pltpu.VMEM((vw, LANES), F32)],
    )
    return pl.pallas_call(
        kern,
        grid_spec=grid_spec,
        out_shape=jax.ShapeDtypeStruct((db, N_HEADS, V_DIM), F32),
        compiler_params=_cparams("arbitrary", "arbitrary"),
        name="decode_attn",
    )(pt_flat, lam, qmat, k_new, v_new, g_subln.reshape(1, V_DIM),
      *([cache_k] * pps), *([cache_v] * pps))


def _gelu_tanh(x):
    return 0.5 * x * (1.0 + jnp.tanh(math.sqrt(2.0 / math.pi) * (x + 0.044715 * (x * x * x))))


def _softplus(z):
    return jnp.maximum(z, 0.0) + jnp.log1p(jnp.exp(-jnp.abs(z)))


def _block_diag(yb, w_ref, b_ref):
    nb = w_ref.shape[0]
    parts = [jnp.dot(yb[:, n * LANES:(n + 1) * LANES], w_ref[n], preferred_element_type=F32)
             for n in range(nb)]
    return jnp.concatenate(parts, axis=1) + b_ref[...]


def _lru_gates(y, wa_ref, ba_ref, wx_ref, bx_ref, lam_ref, reset):
    yb = y.astype(BF16)
    r = jax.nn.sigmoid(_block_diag(yb, wa_ref, ba_ref))
    i = jax.nn.sigmoid(_block_diag(yb, wx_ref, bx_ref))
    log_a = -LRU_C * r * _softplus(-lam_ref[...])
    a = jnp.exp(log_a)
    mult = jnp.sqrt(-jnp.tanh(log_a) * (a * a + 1.0))
    if reset is not None:
        a = jnp.where(reset, 0.0, a)
        mult = jnp.where(reset, 1.0, mult)
    return a, mult * i * y


def _scan_kernel(xr_ref, yr_ref, cs_ref, h0_ref, cw_ref, cb_ref, wa_ref, ba_ref, wx_ref, bx_ref,
                 lam_ref, wo_ref, o_ref, cn_ref, hl_ref, xp_scr, a_scr, b_scr, h_scr, hc_scr,
                 *, tc, reset_first):
    c = pl.program_id(1)

    @pl.when(c == 0)
    def _():
        xp_scr[0:8, :] = cs_ref[0]
        hc_scr[...] = jnp.broadcast_to(h0_ref[0], hc_scr.shape)

    xr = xr_ref[0]
    xp_scr[8:8 + tc, :] = xr
    cw = cw_ref[...]
    nw = cw.shape[0]
    y = cb_ref[...] + cw[nw - 1:nw, :] * xr
    for j in range(nw - 1):
        off = 8 - (nw - 1) + j
        y = y + cw[j:j + 1, :] * xp_scr[off:off + tc, :]
    tail = xp_scr[tc:tc + 8, :]
    xp_scr[0:8, :] = tail
    cn_ref[0] = tail

    reset = None
    if reset_first:
        row = lax.broadcasted_iota(jnp.int32, xr.shape, 0)
        reset = (row + c * tc) == 0
    a, bx = _lru_gates(y, wa_ref, ba_ref, wx_ref, bx_ref, lam_ref, reset)
    a_scr[...] = a
    b_scr[...] = bx

    def body(t, h):
        h = a_scr[pl.ds(t, 1), :] * h + b_scr[pl.ds(t, 1), :]
        h_scr[pl.ds(t, 1), :] = h
        return h

    h_last = lax.fori_loop(0, tc, body, hc_scr[0:1, :], unroll=8)
    hc_scr[...] = jnp.broadcast_to(h_last, hc_scr.shape)
    hl_ref[0] = h_last
    gated = h_scr[...] * _gelu_tanh(yr_ref[0])
    o_ref[0] = jnp.dot(gated.astype(BF16), wo_ref[...], preferred_element_type=F32).astype(BF16)


def _lru_scan(u, conv_state8, h0, conv_w, conv_b, wa, ba, wx, bx, lam, wo, tc, d, reset_first):
    g_, t, _ = u.shape
    nb = wa.shape[0]
    kern = functools.partial(_scan_kernel, tc=tc, reset_first=reset_first)
    vec = lambda: pl.BlockSpec((1, d), lambda g, c: (0, 0))
    return pl.pallas_call(
        kern,
        grid=(g_, t // tc),
        in_specs=[pl.BlockSpec((1, tc, d), lambda g, c: (g, c, 0)),
                  pl.BlockSpec((1, tc, d), lambda g, c: (g, c, 1)),
                  pl.BlockSpec((1, 8, d), lambda g, c: (g, 0, 0)),
                  pl.BlockSpec((1, 1, d), lambda g, c: (g, 0, 0)),
                  pl.BlockSpec(conv_w.shape, lambda g, c: (0, 0)),
                  vec(),
                  pl.BlockSpec(wa.shape, lambda g, c: (0, 0, 0)), vec(),
                  pl.BlockSpec(wx.shape, lambda g, c: (0, 0, 0)), vec(),
                  vec(),
                  pl.BlockSpec(wo.shape, lambda g, c: (0, 0))],
        out_specs=[pl.BlockSpec((1, tc, d), lambda g, c: (g, c, 0)),
                   pl.BlockSpec((1, 8, d), lambda g, c: (g, 0, 0)),
                   pl.BlockSpec((1, 1, d), lambda g, c: (g, 0, 0))],
        out_shape=[jax.ShapeDtypeStruct((g_, t, d), BF16),
                   jax.ShapeDtypeStruct((g_, 8, d), F32),
                   jax.ShapeDtypeStruct((g_, 1, d), F32)],
        scratch_shapes=[pltpu.VMEM((tc + 8, d), F32),
                        pltpu.VMEM((tc, d), F32),
                        pltpu.VMEM((tc, d), F32),
                        pltpu.VMEM((tc, d), F32),
                        pltpu.VMEM((8, d), F32)],
        compiler_params=_cparams("arbitrary", "arbitrary"),
        name="lru_scan",
    )(u, u, conv_state8, h0, conv_w, conv_b.reshape(1, d), wa, ba.reshape(1, d), wx,
      bx.reshape(1, d), lam.reshape(1, d), wo)


def _lru_step_kernel(xr_ref, yr_ref, s_ref, h0_ref, cw_ref, cb_ref, wa_ref, ba_ref, wx_ref, bx_ref,
                     lam_ref, wo_ref, o_ref, hn_ref, *, reset):
    xr = xr_ref[0]
    cw = cw_ref[...]
    nw = cw.shape[0]
    y = cb_ref[...] + cw[nw - 1:nw, :] * xr
    for j in range(nw - 1):
        y = y + cw[j:j + 1, :] * s_ref[j]
    a, bx = _lru_gates(y, wa_ref, ba_ref, wx_ref, bx_ref, lam_ref,
                       jnp.full(y.shape, True) if reset else None)
    h = a * h0_ref[...] + bx
    hn_ref[...] = h
    gated = h * _gelu_tanh(yr_ref[0])
    o_ref[...] = jnp.dot(gated.astype(BF16), wo_ref[...], preferred_element_type=F32).astype(BF16)


def _lru_step(u, conv_state_t, h0, conv_w, conv_b, wa, ba, wx, bx, lam, wo, d, reset):
    m = u.shape[1]
    kern = functools.partial(_lru_step_kernel, reset=reset)
    vec = lambda: pl.BlockSpec((1, d), lambda i: (0, 0))
    return pl.pallas_call(
        kern,
        grid=(1,),
        in_specs=[pl.BlockSpec((1, m, d), lambda i: (0, 0, 0)),
                  pl.BlockSpec((1, m, d), lambda i: (0, 0, 1)),
                  pl.BlockSpec(conv_state_t.shape, lambda i: (0, 0, 0)),
                  pl.BlockSpec((m, d), lambda i: (0, 0)),
                  pl.BlockSpec(conv_w.shape, lambda i: (0, 0)),
                  vec(),
                  pl.BlockSpec(wa.shape, lambda i: (0, 0, 0)), vec(),
                  pl.BlockSpec(wx.shape, lambda i: (0, 0, 0)), vec(),
                  vec(),
                  pl.BlockSpec(wo.shape, lambda i: (0, 0))],
        out_specs=[pl.BlockSpec((m, d), lambda i: (0, 0)),
                   pl.BlockSpec((m, d), lambda i: (0, 0))],
        out_shape=[jax.ShapeDtypeStruct((m, d), BF16),
                   jax.ShapeDtypeStruct((m, d), F32)],
        compiler_params=_cparams("arbitrary"),
        name="lru_step",
    )(u, u, conv_state_t, h0, conv_w, conv_b.reshape(1, d), wa, ba.reshape(1, d), wx,
      bx.reshape(1, d), lam.reshape(1, d), wo)


def _merge_kernel(rnn_ref, attn_ref, gr_ref, ga_ref, x_ref, g1_ref, sc2_ref, sh2_ref,
                  wao_ref, wmo_ref, gpm_ref, gpf_ref, wrh_ref, wrl_ref, br_ref,
                  x1_ref, h2_ref, ti_ref, tg_ref, *, n_exp):
    attn_p = jnp.dot(attn_ref[0], wao_ref[...], preferred_element_type=F32)
    merged = (jax.nn.sigmoid(gr_ref[0]) * rnn_ref[0].astype(F32)
              + jax.nn.sigmoid(ga_ref[0]) * attn_p)
    mix = jnp.dot(merged.astype(BF16), wmo_ref[...], preferred_element_type=F32)
    x1 = x_ref[0] + g1_ref[0] * _rms(mix, gpm_ref[...])
    x1_ref[0] = x1
    h2 = _rms(x1, gpf_ref[...]) * (1.0 + sc2_ref[0]) + sh2_ref[0]
    h2_hi = h2.astype(BF16)
    h2_ref[0] = h2_hi
    h2_lo = (h2 - h2_hi.astype(F32)).astype(BF16)
    wrh = wrh_ref[...]
    logits = (jnp.dot(h2_hi, wrh, preferred_element_type=F32)
              + jnp.dot(h2_lo, wrh, preferred_element_type=F32)
              + jnp.dot(h2_hi, wrl_ref[...], preferred_element_type=F32)) + br_ref[...]
    lane = lax.broadcasted_iota(jnp.int32, logits.shape, 1)
    lane_f = lane.astype(F32)
    l = jnp.where(lane < n_exp, logits, NEG_INF)
    vals, idxs = [], []
    for _ in range(TOP_K):
        mx = jnp.max(l, axis=1, keepdims=True)
        idx = jnp.min(jnp.where(l == mx, lane_f, float(LANES)), axis=1, keepdims=True)
        vals.append(mx)
        idxs.append(idx)
        l = jnp.where(lane_f == idx, NEG_INF, l)
    es = [jnp.exp(v - vals[0]) for v in vals]
    den = es[0]
    for e in es[1:]:
        den = den + e
    ti = jnp.zeros(logits.shape, F32)
    tg = jnp.zeros(logits.shape, F32)
    for k in range(TOP_K):
        ti = jnp.where(lane == k, idxs[k], ti)
        tg = jnp.where(lane == k, es[k] / den, tg)
    ti_ref[0] = ti.astype(jnp.int32)
    tg_ref[0] = tg


def _merge(rnn, attn, u, x, g1, sc2, sh2, wao, wmo, gpm, gpf, wr_hi, wr_lo, br, n_exp, tm,
           gr_blk, ga_blk):
    g_, t, d = x.shape
    kern = functools.partial(_merge_kernel, n_exp=n_exp)
    tok = lambda: pl.BlockSpec((1, tm, d), lambda g, i: (g, i, 0))
    full2 = lambda a: pl.BlockSpec(a.shape, lambda g, i: (0, 0))
    vec = lambda: pl.BlockSpec((1, d), lambda g, i: (0, 0))
    return pl.pallas_call(
        kern,
        grid=(g_, t // tm),
        in_specs=[tok(), tok(),
                  pl.BlockSpec((1, tm, d), lambda g, i: (g, i, gr_blk)),
                  pl.BlockSpec((1, tm, d), lambda g, i: (g, i, ga_blk)),
                  tok(), _mod_spec(g1, tm, 0), _mod_spec(sc2, tm, 0), _mod_spec(sh2, tm, 0),
                  full2(wao), full2(wmo), vec(), vec(), full2(wr_hi), full2(wr_lo), full2(br)],
        out_specs=[tok(), tok(),
                   pl.BlockSpec((1, tm, LANES), lambda g, i: (g, i, 0)),
                   pl.BlockSpec((1, tm, LANES), lambda g, i: (g, i, 0))],
        out_shape=[jax.ShapeDtypeStruct((g_, t, d), F32),
                   jax.ShapeDtypeStruct((g_, t, d), BF16),
                   jax.ShapeDtypeStruct((g_, t, LANES), jnp.int32),
                   jax.ShapeDtypeStruct((g_, t, LANES), F32)],
        compiler_params=_cparams("arbitrary", "arbitrary"),
        name="merge_router",
    )(rnn, attn, u, u, x, g1, sc2, sh2, wao, wmo, gpm.reshape(1, d), gpf.reshape(1, d),
      wr_hi, wr_lo, br)


def _expert_kernel(te_ref, na_ref, x_ref, wg_ref, bg_ref, wu_ref, bu_ref, wd_ref, bd_ref, o_ref,
                   wg_scr, wu_scr, wd_scr):
    t = pl.program_id(0)
    e = te_ref[t]
    prev = te_ref[jnp.maximum(t - 1, 0)]
    active = t < na_ref[0]

    @pl.when(active & ((t == 0) | (e != prev)))
    def _():
        wg_scr[...] = wg_ref[0].astype(BF16)
        wu_scr[...] = wu_ref[0].astype(BF16)
        wd_scr[...] = wd_ref[0].astype(BF16)

    @pl.when(active)
    def _():
        x = x_ref[...]
        glu = jnp.minimum(jnp.dot(x, wg_scr[...], preferred_element_type=F32) + bg_ref[0],
                          SWIGLU_LIMIT)
        lin = jnp.clip(jnp.dot(x, wu_scr[...], preferred_element_type=F32) + bu_ref[0],
                       -SWIGLU_LIMIT, SWIGLU_LIMIT)
        hid = glu * jax.nn.sigmoid(SWIGLU_ALPHA * glu) * (lin + 1.0)
        out = jnp.dot(hid.astype(BF16), wd_scr[...], preferred_element_type=F32) + bd_ref[0]
        o_ref[...] = out.astype(BF16)

    @pl.when(jnp.logical_not(active))
    def _():
        o_ref[...] = jnp.zeros(o_ref.shape, o_ref.dtype)


def _experts(tile_expert, n_active, xs, wg, bg, wu, bu, wd, bd, tm):
    n_rows, d = xs.shape
    n_exp, _, dff = wg.shape
    w_spec = lambda a: pl.BlockSpec((1,) + a.shape[1:], lambda t, te, na: (te[t], 0, 0))
    grid_spec = pltpu.PrefetchScalarGridSpec(
        num_scalar_prefetch=2,
        grid=(n_rows // tm,),
        in_specs=[pl.BlockSpec((tm, d), lambda t, te, na: (t, 0)),
                  w_spec(wg), w_spec(bg), w_spec(wu), w_spec(bu), w_spec(wd), w_spec(bd)],
        out_specs=pl.BlockSpec((tm, d), lambda t, te, na: (t, 0)),
        scratch_shapes=[pltpu.VMEM((d, dff), BF16),
                        pltpu.VMEM((d, dff), BF16),
                        pltpu.VMEM((dff, d), BF16)],
    )
    return pl.pallas_call(
        _expert_kernel,
        grid_spec=grid_spec,
        out_shape=jax.ShapeDtypeStruct((n_rows, d), BF16),
        compiler_params=_cparams("arbitrary"),
        name="expert_ffn",
    )(tile_expert, n_active, xs, wg, bg, wu, bu, wd, bd)


def _combine_kernel(*refs):
    o_refs = refs[:TOP_K]
    tg_ref, x1_ref, g2_ref, gain_ref, out_ref = refs[TOP_K:]
    tg = tg_ref[0]
    y = None
    for k in range(TOP_K):
        term = tg[:, k:k + 1] * o_refs[k][0].astype(F32)
        y = term if y is None else y + term
    out_ref[0] = x1_ref[0] + g2_ref[0] * _rms(y, gain_ref[...])


def _combine(outs, tg, x1, g2, gain, tm):
    g_, t, d = x1.shape
    tok = lambda: pl.BlockSpec((1, tm, d), lambda g, i: (g, i, 0))
    return pl.pallas_call(
        _combine_kernel,
        grid=(g_, t // tm),
        in_specs=[tok() for _ in range(TOP_K)]
                 + [pl.BlockSpec((1, tm, LANES), lambda g, i: (g, i, 0)), tok(),
                    _mod_spec(g2, tm, 0), pl.BlockSpec((1, d), lambda g, i: (0, 0))],
        out_specs=tok(),
        out_shape=jax.ShapeDtypeStruct((g_, t, d), F32),
        compiler_params=_cparams("arbitrary", "arbitrary"),
        name="moe_combine",
    )(*outs, tg, x1, g2, gain.reshape(1, d))


def _route(top_i, n_exp, tm):
    n = top_i.shape[0]
    eid = top_i.reshape(-1)
    onehot = (eid[:, None] == jnp.arange(n_exp, dtype=jnp.int32)[None, :]).astype(jnp.int32)
    csum = jnp.cumsum(onehot, axis=0)
    counts = csum[-1]
    rank = jnp.take_along_axis(csum, eid[:, None], axis=1)[:, 0] - 1
    padded = ((counts + tm - 1) // tm) * tm
    pad_end = jnp.cumsum(padded)
    dest = (pad_end - padded)[eid] + rank
    n_tiles = -(-(n * TOP_K) // tm) + n_exp
    row_token = jnp.zeros((n_tiles * tm,), jnp.int32).at[dest].set(
        jnp.arange(n * TOP_K, dtype=jnp.int32) // TOP_K)
    tile_start = jnp.arange(n_tiles, dtype=jnp.int32) * tm
    tile_expert = jnp.minimum(jnp.searchsorted(pad_end, tile_start, side="right"),
                              n_exp - 1).astype(jnp.int32)
    n_active = (pad_end[-1] // tm).astype(jnp.int32)
    last_e = tile_expert[jnp.maximum(n_active - 1, 0)]
    tile_expert = jnp.where(jnp.arange(n_tiles) < n_active, tile_expert, last_e)
    return dest.reshape(n, TOP_K), row_token, tile_expert, n_active.reshape(1)


def _rope_tables(pos):
    half = HEAD_DIM // 2
    inv = ROPE_THETA ** (-jnp.arange(half, dtype=F32) / half)
    ang = pos.astype(F32)[:, None] * inv[None, :]
    cos, sin = jnp.cos(ang), jnp.sin(ang)
    reps = LANES // HEAD_DIM
    cos_t = jnp.tile(cos, (1, 2 * reps))
    sin_t = jnp.tile(jnp.concatenate([-sin, sin], axis=1), (1, reps))
    return cos_t, sin_t


def _decode_qmat(q_bf):
    db = q_bf.shape[0]
    n_rep = N_HEADS // N_KV_HEADS
    q5 = q_bf.reshape(db, N_KV_HEADS, n_rep, 2, HEAD_DIM).transpose(0, 1, 3, 4, 2)
    eye_m = jnp.eye(2, dtype=q_bf.dtype)
    eye_j = jnp.eye(N_KV_HEADS, dtype=q_bf.dtype)
    t = (q5[:, :, :, :, None, None, :]
         * eye_m[None, None, :, None, :, None, None]
         * eye_j[None, :, None, None, None, :, None])
    qm = t.reshape(db, N_KV_HEADS * 2 * HEAD_DIM, 2 * N_HEADS)
    return jnp.pad(qm, ((0, 0), (0, 0), (0, LANES - 2 * N_HEADS)))


def _layer(xp, xs, c_all, cache_k, cache_v, conv_state, h_state, page_table, p, lam_init):
    b, s, d = xp.shape
    db = xs.shape[0]
    past = page_table.shape[1] * cache_k.shape[1]
    d_rnn = p["conv_w"].shape[1]
    q_w = N_HEADS * 2 * HEAD_DIM
    k_w = N_KV_HEADS * 2 * HEAD_DIM
    v_w = N_KV_HEADS * V_DIM
    q_off = 2 * d_rnn
    k_off = q_off + q_w
    v_off = k_off + k_w
    gr_off = v_off + v_w
    ga_off = gr_off + d
    n_exp = p["w_router"].shape[1]
    tm = 512

    mod = _ada(c_all, p["w_ada"], p["b_ada"])
    mod_s = [m.reshape(1, db, d) for m in jnp.split(mod[:db], 6, axis=-1)]
    mod_p = [m.reshape(b, 1, d) for m in jnp.split(mod[db:db + b], 6, axis=-1)]

    w_in = p["w_in"].astype(BF16)
    wa = p["w_gate_a"].astype(BF16)
    wx = p["w_gate_x"].astype(BF16)
    w_rnn_out = p["w_rnn_out"].astype(BF16)
    w_attn_out = p["w_attn_out"].astype(BF16)
    w_mix_out = p["w_mix_out"].astype(BF16)
    wr = jnp.pad(p["w_router"], ((0, 0), (0, LANES - n_exp)))
    wr_hi = wr.astype(BF16)
    wr_lo = (wr - wr_hi.astype(F32)).astype(BF16)
    br = jnp.pad(p["b_router"], (0, LANES - n_exp)).reshape(1, LANES)
    lam = (jnp.exp(jnp.sum(p["lambda_q1"] * p["lambda_k1"]))
           - jnp.exp(jnp.sum(p["lambda_q2"] * p["lambda_k2"])) + lam_init).reshape(1).astype(F32)

    xs_g = xs.reshape(1, db, d)

    u_p = _inproj(xp, mod_p[1], mod_p[0], p["g_pre_mix"], w_in, tm)
    u_s = _inproj(xs_g, mod_s[1], mod_s[0], p["g_pre_mix"], w_in, db)

    cos_p, sin_p = _rope_tables(jnp.arange(s, dtype=jnp.int32))
    cos_s, sin_s = _rope_tables(jnp.full((db,), past, dtype=jnp.int32))
    qb_p, kf_p, kb_p, vf_p, vb_p = _prep(u_p, cos_p, sin_p, tm, q_w, k_w, v_w, q_off, k_off, v_off)
    qb_s, kf_s, kb_s, vf_s, vb_s = _prep(u_s, cos_s, sin_s, db, q_w, k_w, v_w, q_off, k_off, v_off)

    attn_p = _flash(lam, qb_p, kb_p, vb_p, p["g_subln"], lam_init, 256)
    qmat = _decode_qmat(qb_s[0])
    k_new = jnp.pad(kb_s[0][:, None, :], ((0, 0), (0, NEW_ROWS - 1), (0, 0)))
    v_new = jnp.pad(vb_s[0][:, None, :], ((0, 0), (0, NEW_ROWS - 1), (0, 0)))
    attn_s = _decode(page_table, lam, qmat, k_new, v_new, p["g_subln"], cache_k, cache_v,
                     lam_init, 8).reshape(1, db, N_HEADS * V_DIM).astype(BF16)

    nw = p["conv_w"].shape[0]
    zero_conv = jnp.zeros((b, 8, d_rnn), F32)
    zero_h = jnp.zeros((b, 1, d_rnn), F32)
    rnn_p, conv_p8, h_p = _lru_scan(u_p, zero_conv, zero_h, p["conv_w"], p["conv_b"], wa,
                                    p["b_gate_a"], wx, p["b_gate_x"], p["lru_lambda"], w_rnn_out,
                                    256, d_rnn, True)
    conv_p = conv_p8[:, 8 - (nw - 1):, :]
    conv_state_t = jnp.transpose(conv_state, (1, 0, 2))
    rnn_s, h_s = _lru_step(u_s, conv_state_t, h_state, p["conv_w"], p["conv_b"], wa, p["b_gate_a"],
                           wx, p["b_gate_x"], p["lru_lambda"], w_rnn_out, d_rnn, past == 0)
    conv_s = jnp.concatenate([conv_state[:, 1:, :], u_s[0][:, None, :d_rnn]], axis=1)

    x1_p, h2_p, ti_p, tg_p = _merge(rnn_p, attn_p, u_p, xp, mod_p[2], mod_p[4], mod_p[3],
                                    w_attn_out, w_mix_out, p["g_post_mix"], p["g_pre_ffn"],
                                    wr_hi, wr_lo, br, n_exp, tm, gr_off // d, ga_off // d)
    x1_s, h2_s, ti_s, tg_s = _merge(rnn_s.reshape(1, db, d), attn_s, u_s, xs_g, mod_s[2], mod_s[4],
                                    mod_s[3], w_attn_out, w_mix_out, p["g_post_mix"],
                                    p["g_pre_ffn"], wr_hi, wr_lo, br, n_exp, db,
                                    gr_off // d, ga_off // d)

    n_p = b * s
    h2_all = jnp.concatenate([h2_p.reshape(n_p, d), h2_s.reshape(db, d)], axis=0)
    ti_all = jnp.concatenate([ti_p.reshape(n_p, LANES), ti_s.reshape(db, LANES)], axis=0)[:, :TOP_K]
    te_tm = 256
    dest, row_token, tile_expert, n_active = _route(ti_all, n_exp, te_tm)
    x_sorted = jnp.take(h2_all, row_token, axis=0)
    o_sorted = _experts(tile_expert, n_active, x_sorted, p["w_gate_e"],
                        p["b_gate_e"][:, None, :], p["w_up_e"], p["b_up_e"][:, None, :],
                        p["w_down_e"], p["b_down_e"][:, None, :], te_tm)
    outs_p = [jnp.take(o_sorted, dest[:n_p, k], axis=0).reshape(b, s, d) for k in range(TOP_K)]
    outs_s = [jnp.take(o_sorted, dest[n_p:, k], axis=0).reshape(1, db, d) for k in range(TOP_K)]
    y_p = _combine(outs_p, tg_p, x1_p, mod_p[5], p["g_post_ffn"], tm)
    y_s = _combine(outs_s, tg_s, x1_s, mod_s[5], p["g_post_ffn"], db)

    state = (kf_p.reshape(b, s, N_KV_HEADS, 2 * HEAD_DIM), vf_p.reshape(b, s, N_KV_HEADS, V_DIM),
             kf_s.reshape(db, 1, N_KV_HEADS, 2 * HEAD_DIM), vf_s.reshape(db, 1, N_KV_HEADS, V_DIM),
             conv_p, conv_s, h_p.reshape(b, d_rnn), h_s)
    return y_p, y_s.reshape(db, 1, d), state


def kernel(x_prompt, x_sample, cache_k, cache_v, state_conv, state_rglru, page_table, c_prompt, c_sample, w_ada, b_ada, g_pre_mix, g_post_mix, g_pre_ffn, g_post_ffn, w_in, conv_w, conv_b, w_gate_a, b_gate_a, w_gate_x, b_gate_x, lru_lambda, w_rnn_out, lambda_q1, lambda_k1, lambda_q2, lambda_k2, g_subln, w_attn_out, w_mix_out, w_router, b_router, w_gate_e, b_gate_e, w_up_e, b_up_e, w_down_e, b_down_e):
    depth = w_in.shape[0]
    db, dec_seq, d = x_sample.shape
    assert dec_seq == 1
    b = x_prompt.shape[0]
    n_pool, page = cache_k.shape[1], cache_k.shape[2]
    pad_rows = (-(db + b)) % 8
    c_all = jnp.concatenate([c_sample, c_prompt, jnp.zeros((pad_rows, d), F32)], axis=0)
    xp, xs = x_prompt, x_sample
    states = []
    params = dict(w_ada=w_ada, b_ada=b_ada, g_pre_mix=g_pre_mix, g_post_mix=g_post_mix,
                  g_pre_ffn=g_pre_ffn, g_post_ffn=g_post_ffn, w_in=w_in, conv_w=conv_w,
                  conv_b=conv_b, w_gate_a=w_gate_a, b_gate_a=b_gate_a, w_gate_x=w_gate_x,
                  b_gate_x=b_gate_x, lru_lambda=lru_lambda, w_rnn_out=w_rnn_out,
                  lambda_q1=lambda_q1, lambda_k1=lambda_k1, lambda_q2=lambda_q2,
                  lambda_k2=lambda_k2, g_subln=g_subln, w_attn_out=w_attn_out,
                  w_mix_out=w_mix_out, w_router=w_router, b_router=b_router,
                  w_gate_e=w_gate_e, b_gate_e=b_gate_e, w_up_e=w_up_e, b_up_e=b_up_e,
                  w_down_e=w_down_e, b_down_e=b_down_e)
    take = (lambda a, l: a.reshape(a.shape[1:])) if depth == 1 else (lambda a, l: a[l])
    for l in range(depth):
        p = {name: take(a, l) for name, a in params.items()}
        lam_init = 0.8 - 0.6 * math.exp(-0.3 * l)
        ck = take(cache_k, l).reshape(n_pool, page, -1)
        cv = take(cache_v, l).reshape(n_pool, page, -1)
        xp, xs_new, st = _layer(xp, xs.reshape(db, d), c_all, ck, cv, take(state_conv, l),
                                take(state_rglru, l), page_table, p, lam_init)
        xs = xs_new
        states.append(st)
    stacked = [jnp.stack([st[i] for st in states]) for i in range(8)]
    return (xp, xs, *stacked)
```

```python
import functools
import math

import jax
import jax.numpy as jnp
from jax import lax
from jax.experimental import pallas as pl
from jax.experimental.pallas import tpu as pltpu

F32 = jnp.float32
BF16 = jnp.bfloat16

N_HEADS = 8
N_KV_HEADS = 4
HEAD_DIM = 64
V_DIM = 2 * HEAD_DIM
LRU_C = 8.0
ROPE_THETA = 10000.0
TOP_K = 4
SWIGLU_ALPHA = 1.702
SWIGLU_LIMIT = 7.0
EPS = 1e-6

LANES = 128
VMEM_LIMIT = 56 * 1024 * 1024
NEG_INF = float("-inf")


def _cparams(*sem):
    return pltpu.CompilerParams(dimension_semantics=sem, vmem_limit_bytes=VMEM_LIMIT)


def _rms(x, g):
    return x * lax.rsqrt(jnp.mean(x * x, axis=-1, keepdims=True) + EPS) * g


def _ada_kernel(c_ref, w_ref, b_ref, o_ref):
    c = c_ref[...]
    s = (c * jax.nn.sigmoid(c)).astype(BF16)
    o_ref[...] = jnp.dot(s, w_ref[...].astype(BF16), preferred_element_type=F32) + b_ref[...]


def _ada(c, w, b):
    m, d = c.shape
    n = w.shape[1]
    tn = 1024
    return pl.pallas_call(
        _ada_kernel,
        grid=(n // tn,),
        in_specs=[pl.BlockSpec((m, d), lambda j: (0, 0)),
                  pl.BlockSpec((d, tn), lambda j: (0, j)),
                  pl.BlockSpec((1, tn), lambda j: (0, j))],
        out_specs=pl.BlockSpec((m, tn), lambda j: (0, j)),
        out_shape=jax.ShapeDtypeStruct((m, n), F32),
        compiler_params=_cparams("arbitrary"),
        name="ada_mod",
    )(c, w, b.reshape(1, n))


def _inproj_kernel(x_ref, sc_ref, sh_ref, g_ref, w_ref, o_ref):
    h = _rms(x_ref[0], g_ref[...])
    h = h * (1.0 + sc_ref[0]) + sh_ref[0]
    o_ref[0] = jnp.dot(h.astype(BF16), w_ref[...], preferred_element_type=F32)


def _mod_spec(mod, tm, n_lead):
    per_row = mod.shape[1] != 1
    d = mod.shape[2]
    rows = tm if per_row else 1
    if n_lead == 1:
        return pl.BlockSpec((1, rows, d), lambda n, g, i: (g, i if per_row else 0, 0))
    return pl.BlockSpec((1, rows, d), lambda g, i: (g, i if per_row else 0, 0))


def _inproj(x, sc, sh, gain, w_bf, tm):
    g_, t, d = x.shape
    n = w_bf.shape[1]
    tn = 2048
    return pl.pallas_call(
        _inproj_kernel,
        grid=(n // tn, g_, t // tm),
        in_specs=[pl.BlockSpec((1, tm, d), lambda n, g, i: (g, i, 0)),
                  _mod_spec(sc, tm, 1), _mod_spec(sh, tm, 1),
                  pl.BlockSpec((1, d), lambda n, g, i: (0, 0)),
                  pl.BlockSpec((d, tn), lambda n, g, i: (0, n))],
        out_specs=pl.BlockSpec((1, tm, tn), lambda n, g, i: (g, i, n)),
        out_shape=jax.ShapeDtypeStruct((g_, t, n), F32),
        compiler_params=_cparams("arbitrary", "arbitrary", "arbitrary"),
        name="in_proj",
    )(x, sc, sh, gain.reshape(1, d), w_bf)


def _rope_heads(x, c, s, first_half):
    outs = []
    for h in range(x.shape[1] // LANES):
        xh = x[:, h * LANES:(h + 1) * LANES]
        swapped = jnp.where(first_half, pltpu.roll(xh, LANES - HEAD_DIM // 2, 1),
                            pltpu.roll(xh, HEAD_DIM // 2, 1))
        outs.append(xh * c + swapped * s)
    return jnp.concatenate(outs, axis=1)


def _prep_kernel(q_ref, k_ref, v_ref, c_ref, s_ref, qb_ref, kf_ref, kb_ref, vf_ref, vb_ref):
    c = c_ref[...]
    s = s_ref[...]
    lane = lax.broadcasted_iota(jnp.int32, c.shape, 1)
    first_half = (lane & (HEAD_DIM - 1)) < HEAD_DIM // 2
    q = _rope_heads(q_ref[0], c, s, first_half) * (HEAD_DIM ** -0.5)
    k = _rope_heads(k_ref[0], c, s, first_half)
    v = v_ref[0]
    qb_ref[0] = q.astype(BF16)
    kf_ref[0] = k
    kb_ref[0] = k.astype(BF16)
    vf_ref[0] = v
    vb = v.astype(BF16)
    ones = jnp.ones((v.shape[0], V_DIM), BF16)
    parts = []
    for j in range(v.shape[1] // V_DIM):
        parts += [vb[:, j * V_DIM:(j + 1) * V_DIM], ones]
    vb_ref[0] = jnp.concatenate(parts, axis=1)


def _prep(u, cos_t, sin_t, tm, q_w, k_w, v_w, q_off, k_off, v_off):
    g_, t, _ = u.shape
    return pl.pallas_call(
        _prep_kernel,
        grid=(g_, t // tm),
        in_specs=[pl.BlockSpec((1, tm, q_w), lambda g, i: (g, i, q_off // q_w)),
                  pl.BlockSpec((1, tm, k_w), lambda g, i: (g, i, k_off // k_w)),
                  pl.BlockSpec((1, tm, v_w), lambda g, i: (g, i, v_off // v_w)),
                  pl.BlockSpec((tm, LANES), lambda g, i: (i, 0)),
                  pl.BlockSpec((tm, LANES), lambda g, i: (i, 0))],
        out_specs=[pl.BlockSpec((1, tm, q_w), lambda g, i: (g, i, 0)),
                   pl.BlockSpec((1, tm, k_w), lambda g, i: (g, i, 0)),
                   pl.BlockSpec((1, tm, k_w), lambda g, i: (g, i, 0)),
                   pl.BlockSpec((1, tm, v_w), lambda g, i: (g, i, 0)),
                   pl.BlockSpec((1, tm, 2 * v_w), lambda g, i: (g, i, 0))],
        out_shape=[jax.ShapeDtypeStruct((g_, t, q_w), BF16),
                   jax.ShapeDtypeStruct((g_, t, k_w), F32),
                   jax.ShapeDtypeStruct((g_, t, k_w), BF16),
                   jax.ShapeDtypeStruct((g_, t, v_w), F32),
                   jax.ShapeDtypeStruct((g_, t, 2 * v_w), BF16)],
        compiler_params=_cparams("arbitrary", "arbitrary"),
        name="qkv_prep",
    )(u, u, u, cos_t, sin_t)


def _flash_kernel(lam_ref, q_ref, k_ref, v_ref, gs_ref, o_ref, q4_scr, m_scr, acc_scr,
                  *, tq, lam_init):
    i = pl.program_id(2)
    n_rep = N_HEADS // N_KV_HEADS
    q = q_ref[0]
    lane = lax.broadcasted_iota(jnp.int32, (tq, LANES), 1)
    is_map1 = lane < HEAD_DIM
    zero = jnp.zeros((tq, LANES), BF16)
    for g in range(n_rep):
        qg = q[:, g * LANES:(g + 1) * LANES]
        q4_scr[(2 * g) * tq:(2 * g + 1) * tq, :] = jnp.where(is_map1, qg, zero)
        q4_scr[(2 * g + 1) * tq:(2 * g + 2) * tq, :] = jnp.where(is_map1, zero, qg)
    m_scr[...] = jnp.full(m_scr.shape, NEG_INF, F32)
    acc_scr[...] = jnp.zeros(acc_scr.shape, F32)
    rows = 2 * n_rep * tq

    def step(start, size, masked):
        k = k_ref[0, pl.ds(start, size), :]
        v = v_ref[0, pl.ds(start, size), :]
        for r0 in range(0, rows, tq):
            rs = slice(r0, r0 + tq)
            s = lax.dot_general(q4_scr[rs, :], k, (((1,), (1,)), ((), ())),
                                preferred_element_type=F32)
            if masked:
                r = lax.broadcasted_iota(jnp.int32, (tq, size), 0)
                c = lax.broadcasted_iota(jnp.int32, (tq, size), 1)
                s = jnp.where(r >= c, s, NEG_INF)
            m_prev = m_scr[rs, :]
            m_new = jnp.maximum(m_prev, jnp.max(s, axis=1, keepdims=True))
            alpha = jnp.exp(m_prev - m_new)
            p = jnp.exp(s - jnp.concatenate([m_new] * (size // LANES), axis=1))
            acc_scr[rs, :] = (jnp.concatenate([alpha, alpha], axis=1) * acc_scr[rs, :]
                              + jnp.dot(p.astype(BF16), v, preferred_element_type=F32))
            m_scr[rs, :] = m_new

    def body(t, carry):
        step(pl.multiple_of(t * 2 * tq, 2 * tq), 2 * tq, False)
        return carry

    lax.fori_loop(0, i // 2, body, 0)

    @pl.when(i % 2 == 1)
    def _():
        step(pl.multiple_of((i - 1) * tq, tq), tq, False)

    step(pl.multiple_of(i * tq, tq), tq, True)

    o = acc_scr[:, 0:V_DIM] / acc_scr[:, V_DIM:2 * V_DIM]
    lam = lam_ref[0]
    for g in range(n_rep):
        o1 = o[(2 * g) * tq:(2 * g + 1) * tq, :]
        o2 = o[(2 * g + 1) * tq:(2 * g + 2) * tq, :]
        d = _rms(o1 - lam * o2, gs_ref[...]) * (1.0 - lam_init)
        o_ref[0, :, g * LANES:(g + 1) * LANES] = d.astype(BF16)


def _flash(lam, q_bf, k_bf, v_bf, g_subln, lam_init, tq):
    b, s, _ = q_bf.shape
    n_rep = N_HEADS // N_KV_HEADS
    qw = n_rep * 2 * HEAD_DIM
    rows = 2 * n_rep * tq
    kern = functools.partial(_flash_kernel, tq=tq, lam_init=lam_init)
    return pl.pallas_call(
        kern,
        grid=(b, N_KV_HEADS, s // tq),
        in_specs=[pl.BlockSpec(memory_space=pltpu.SMEM),
                  pl.BlockSpec((1, tq, qw), lambda b_, j, i: (b_, i, j)),
                  pl.BlockSpec((1, s, 2 * HEAD_DIM), lambda b_, j, i: (b_, 0, j)),
                  pl.BlockSpec((1, s, 2 * V_DIM), lambda b_, j, i: (b_, 0, j)),
                  pl.BlockSpec((1, V_DIM), lambda b_, j, i: (0, 0))],
        out_specs=pl.BlockSpec((1, tq, n_rep * V_DIM), lambda b_, j, i: (b_, i, j)),
        out_shape=jax.ShapeDtypeStruct((b, s, N_HEADS * V_DIM), BF16),
        scratch_shapes=[pltpu.VMEM((rows, LANES), BF16),
                        pltpu.VMEM((rows, LANES), F32),
                        pltpu.VMEM((rows, 2 * V_DIM), F32)],
        compiler_params=_cparams("arbitrary", "arbitrary", "arbitrary"),
        name="prompt_attn",
    )(lam, q_bf, k_bf, v_bf, g_subln.reshape(1, V_DIM))


def _decode_kernel(pt_ref, lam_ref, qt_ref, kn_ref, vn_ref, gs_ref, *rest, pps, lam_init):
    k_refs = rest[:pps]
    v_refs = rest[pps:2 * pps]
    o_ref = rest[2 * pps]
    m_scr, l_scr, acc_scr = rest[2 * pps + 1:]
    s_id = pl.program_id(1)
    n_steps = pl.num_programs(1)

    @pl.when(s_id == 0)
    def _():
        m_scr[...] = jnp.full(m_scr.shape, NEG_INF, F32)
        l_scr[...] = jnp.zeros(l_scr.shape, F32)
        acc_scr[...] = jnp.zeros(acc_scr.shape, F32)

    n_rep = N_HEADS // N_KV_HEADS
    n_str = 2 * N_HEADS
    page = k_refs[0].shape[1] // N_KV_HEADS
    row_kv = (lax.broadcasted_iota(jnp.int32, (n_str, LANES), 0) & (N_HEADS - 1)) // n_rep

    def kv_rows(refs, j):
        return jnp.concatenate([r[0, pl.ds(j, page, stride=N_KV_HEADS), :] for r in refs],
                               axis=0).astype(BF16)

    s = None
    for j in range(N_KV_HEADS):
        sj = lax.dot_general(qt_ref[0, j], kv_rows(k_refs, j), (((1,), (1,)), ((), ())),
                             preferred_element_type=F32)
        s = sj if s is None else s + sj
    m_prev = m_scr[...]
    m_new = jnp.maximum(m_prev, jnp.max(s, axis=1, keepdims=True))
    alpha = jnp.exp(m_prev - m_new)
    p = jnp.exp(s - jnp.concatenate([m_new] * (s.shape[1] // LANES), axis=1))
    l_scr[...] = alpha * l_scr[...] + jnp.sum(p, axis=1, keepdims=True)
    pb = p.astype(BF16)
    acc = alpha * acc_scr[...]
    for j in range(N_KV_HEADS):
        oj = jnp.dot(pb, kv_rows(v_refs, j), preferred_element_type=F32)
        acc = acc + jnp.where(row_kv == j, oj, 0.0)
    acc_scr[...] = acc
    m_scr[...] = m_new

    @pl.when(s_id == n_steps - 1)
    def _():
        qt = qt_ref[0, 0].astype(F32)
        for j in range(1, N_KV_HEADS):
            qt = qt + qt_ref[0, j].astype(F32)
        s_self = jnp.sum(qt * kn_ref[0], axis=1, keepdims=True)
        m_last = m_scr[...]
        m_fin = jnp.maximum(m_last, s_self)
        a_fin = jnp.exp(m_last - m_fin)
        p_self = jnp.exp(s_self - m_fin)
        l_fin = a_fin * l_scr[...] + p_self
        o = (a_fin * acc_scr[...] + p_self * vn_ref[0]) / l_fin
        d = o[0:N_HEADS, :] - lam_ref[0] * o[N_HEADS:n_str, :]
        o_ref[0] = _rms(d, gs_ref[...]) * (1.0 - lam_init)


def _decode(page_table, lam, qmat, k_new, v_new, g_subln, cache_k, cache_v, lam_init, pps):
    db, n_pages = page_table.shape
    n_pool, page_rows, kw = cache_k.shape
    vw = cache_v.shape[2]
    n_str = 2 * N_HEADS
    pt_flat = page_table.reshape(-1)

    def page_spec(width, i):
        return pl.BlockSpec((1, page_rows, width),
                            lambda b, s, pt: (pt[b * n_pages + s * pps + i], 0, 0))

    kern = functools.partial(_decode_kernel, pps=pps, lam_init=lam_init)
    grid_spec = pltpu.PrefetchScalarGridSpec(
        num_scalar_prefetch=1,
        grid=(db, n_pages // pps),
        in_specs=[pl.BlockSpec(memory_space=pltpu.SMEM),
                  pl.BlockSpec((1, N_KV_HEADS, n_str, kw), lambda b, s, pt: (b, 0, 0, 0)),
                  pl.BlockSpec((1, n_str, kw), lambda b, s, pt: (b, 0, 0)),
                  pl.BlockSpec((1, n_str, vw), lambda b, s, pt: (b, 0, 0)),
                  pl.BlockSpec((1, V_DIM), lambda b, s, pt: (0, 0))]
                 + [page_spec(kw, i) for i in range(pps)]
                 + [page_spec(vw, i) for i in range(pps)],
        out_specs=pl.BlockSpec((1, N_HEADS, V_DIM), lambda b, s, pt: (b, 0, 0)),
        scratch_shapes=[pltpu.VMEM((n_str, LANES), F32),
                        pltpu.VMEM((n_str, LANES), F32),
                        pltpu.VMEM((n_str, vw), F32)],
    )
    return pl.pallas_call(
        kern,
        grid_spec=grid_spec,
        out_shape=jax.ShapeDtypeStruct((db, N_HEADS, V_DIM), F32),
        compiler_params=_cparams("arbitrary", "arbitrary"),
        name="decode_attn",
    )(pt_flat, lam, qmat, k_new, v_new, g_subln.reshape(1, V_DIM),
      *([cache_k] * pps), *([cache_v] * pps))


def _gelu_tanh(x):
    return 0.5 * x * (1.0 + jnp.tanh(math.sqrt(2.0 / math.pi) * (x + 0.044715 * (x * x * x))))


def _softplus(z):
    return jnp.maximum(z, 0.0) + jnp.log1p(jnp.exp(-jnp.abs(z)))


def _block_diag(yb, w_ref, b_ref):
    nb = w_ref.shape[0]
    parts = [jnp.dot(yb[:, n * LANES:(n + 1) * LANES], w_ref[n], preferred_element_type=F32)
             for n in range(nb)]
    return jnp.concatenate(parts, axis=1) + b_ref[...]


def _lru_gates(y, wa_ref, ba_ref, wx_ref, bx_ref, lam_ref, reset):
    yb = y.astype(BF16)
    r = jax.nn.sigmoid(_block_diag(yb, wa_ref, ba_ref))
    i = jax.nn.sigmoid(_block_diag(yb, wx_ref, bx_ref))
    log_a = -LRU_C * r * _softplus(-lam_ref[...])
    a = jnp.exp(log_a)
    mult = jnp.sqrt(-jnp.tanh(log_a) * (a * a + 1.0))
    if reset is not None:
        a = jnp.where(reset, 0.0, a)
        mult = jnp.where(reset, 1.0, mult)
    return a, mult * i * y


def _scan_kernel(xr_ref, yr_ref, cs_ref, h0_ref, cw_ref, cb_ref, wa_ref, ba_ref, wx_ref, bx_ref,
                 lam_ref, wo_ref, o_ref, cn_ref, hl_ref, xp_scr, a_scr, b_scr, h_scr, hc_scr,
                 *, tc, reset_first):
    c = pl.program_id(1)

    @pl.when(c == 0)
    def _():
        xp_scr[0:8, :] = cs_ref[0]
        hc_scr[...] = jnp.broadcast_to(h0_ref[0], hc_scr.shape)

    xr = xr_ref[0]
    xp_scr[8:8 + tc, :] = xr
    cw = cw_ref[...]
    nw = cw.shape[0]
    y = cb_ref[...] + cw[nw - 1:nw, :] * xr
    for j in range(nw - 1):
        off = 8 - (nw - 1) + j
        y = y + cw[j:j + 1, :] * xp_scr[off:off + tc, :]
    tail = xp_scr[tc:tc + 8, :]
    xp_scr[0:8, :] = tail
    cn_ref[0] = tail

    reset = None
    if reset_first:
        row = lax.broadcasted_iota(jnp.int32, xr.shape, 0)
        reset = (row + c * tc) == 0
    a, bx = _lru_gates(y, wa_ref, ba_ref, wx_ref, bx_ref, lam_ref, reset)
    a_scr[...] = a
    b_scr[...] = bx

    def body(t, h):
        h = a_scr[pl.ds(t, 1), :] * h + b_scr[pl.ds(t, 1), :]
        h_scr[pl.ds(t, 1), :] = h
        return h

    h_last = lax.fori_loop(0, tc, body, hc_scr[0:1, :], unroll=8)
    hc_scr[...] = jnp.broadcast_to(h_last, hc_scr.shape)
    hl_ref[0] = h_last
    gated = h_scr[...] * _gelu_tanh(yr_ref[0])
    o_ref[0] = jnp.dot(gated.astype(BF16), wo_ref[...], preferred_element_type=F32).astype(BF16)


def _lru_scan(u, conv_state8, h0, conv_w, conv_b, wa, ba, wx, bx, lam, wo, tc, d, reset_first):
    g_, t, _ = u.shape
    nb = wa.shape[0]
    kern = functools.partial(_scan_kernel, tc=tc, reset_first=reset_first)
    vec = lambda: pl.BlockSpec((1, d), lambda g, c: (0, 0))
    return pl.pallas_call(
        kern,
        grid=(g_, t // tc),
        in_specs=[pl.BlockSpec((1, tc, d), lambda g, c: (g, c, 0)),
                  pl.BlockSpec((1, tc, d), lambda g, c: (g, c, 1)),
                  pl.BlockSpec((1, 8, d), lambda g, c: (g, 0, 0)),
                  pl.BlockSpec((1, 1, d), lambda g, c: (g, 0, 0)),
                  pl.BlockSpec(conv_w.shape, lambda g, c: (0, 0)),
                  vec(),
                  pl.BlockSpec(wa.shape, lambda g, c: (0, 0, 0)), vec(),
                  pl.BlockSpec(wx.shape, lambda g, c: (0, 0, 0)), vec(),
                  vec(),
                  pl.BlockSpec(wo.shape, lambda g, c: (0, 0))],
        out_specs=[pl.BlockSpec((1, tc, d), lambda g, c: (g, c, 0)),
                   pl.BlockSpec((1, 8, d), lambda g, c: (g, 0, 0)),
                   pl.BlockSpec((1, 1, d), lambda g, c: (g, 0, 0))],
        out_shape=[jax.ShapeDtypeStruct((g_, t, d), BF16),
                   jax.ShapeDtypeStruct((g_, 8, d), F32),
                   jax.ShapeDtypeStruct((g_, 1, d), F32)],
        scratch_shapes=[pltpu.VMEM((tc + 8, d), F32),
                        pltpu.VMEM((tc, d), F32),
                        pltpu.VMEM((tc, d), F32),
                        pltpu.VMEM((tc, d), F32),
                        pltpu.VMEM((8, d), F32)],
        compiler_params=_cparams("arbitrary", "arbitrary"),
        name="lru_scan",
    )(u, u, conv_state8, h0, conv_w, conv_b.reshape(1, d), wa, ba.reshape(1, d), wx,
      bx.reshape(1, d), lam.reshape(1, d), wo)


def _lru_step_kernel(xr_ref, yr_ref, s_ref, h0_ref, cw_ref, cb_ref, wa_ref, ba_ref, wx_ref, bx_ref,
                     lam_ref, wo_ref, o_ref, hn_ref, *, reset):
    xr = xr_ref[0]
    cw = cw_ref[...]
    nw = cw.shape[0]
    y = cb_ref[...] + cw[nw - 1:nw, :] * xr
    for j in range(nw - 1):
        y = y + cw[j:j + 1, :] * s_ref[j]
    a, bx = _lru_gates(y, wa_ref, ba_ref, wx_ref, bx_ref, lam_ref,
                       jnp.full(y.shape, True) if reset else None)
    h = a * h0_ref[...] + bx
    hn_ref[...] = h
    gated = h * _gelu_tanh(yr_ref[0])
    o_ref[...] = jnp.dot(gated.astype(BF16), wo_ref[...], preferred_element_type=F32).astype(BF16)


def _lru_step(u, conv_state_t, h0, conv_w, conv_b, wa, ba, wx, bx, lam, wo, d, reset):
    m = u.shape[1]
    kern = functools.partial(_lru_step_kernel, reset=reset)
    vec = lambda: pl.BlockSpec((1, d), lambda i: (0, 0))
    return pl.pallas_call(
        kern,
        grid=(1,),
        in_specs=[pl.BlockSpec((1, m, d), lambda i: (0, 0, 0)),
                  pl.BlockSpec((1, m, d), lambda i: (0, 0, 1)),
                  pl.BlockSpec(conv_state_t.shape, lambda i: (0, 0, 0)),
                  pl.BlockSpec((m, d), lambda i: (0, 0)),
                  pl.BlockSpec(conv_w.shape, lambda i: (0, 0)),
                  vec(),
                  pl.BlockSpec(wa.shape, lambda i: (0, 0, 0)), vec(),
                  pl.BlockSpec(wx.shape, lambda i: (0, 0, 0)), vec(),
                  vec(),
                  pl.BlockSpec(wo.shape, lambda i: (0, 0))],
        out_specs=[pl.BlockSpec((m, d), lambda i: (0, 0)),
                   pl.BlockSpec((m, d), lambda i: (0, 0))],
        out_shape=[jax.ShapeDtypeStruct((m, d), BF16),
                   jax.ShapeDtypeStruct((m, d), F32)],
        compiler_params=_cparams("arbitrary"),
        name="lru_step",
    )(u, u, conv_state_t, h0, conv_w, conv_b.reshape(1, d), wa, ba.reshape(1, d), wx,
      bx.reshape(1, d), lam.reshape(1, d), wo)


def _merge_kernel(rnn_ref, attn_ref, gr_ref, ga_ref, x_ref, g1_ref, sc2_ref, sh2_ref,
                  wao_ref, wmo_ref, gpm_ref, gpf_ref, wrh_ref, wrl_ref, br_ref, *rest, n_exp):
    x1_ref, h2_ref, ti_ref, tg_ref = rest[-4:]
    attn_p = jnp.dot(attn_ref[0], wao_ref[...], preferred_element_type=F32)
    merged = (jax.nn.sigmoid(gr_ref[0]) * rnn_ref[0].astype(F32)
              + jax.nn.sigmoid(ga_ref[0]) * attn_p)
    mix = jnp.dot(merged.astype(BF16), wmo_ref[...], preferred_element_type=F32)
    x1 = x_ref[0] + g1_ref[0] * _rms(mix, gpm_ref[...])
    x1_ref[0] = x1
    h2 = _rms(x1, gpf_ref[...]) * (1.0 + sc2_ref[0]) + sh2_ref[0]
    h2_hi = h2.astype(BF16)
    h2_ref[...] = h2_hi
    h2_lo = (h2 - h2_hi.astype(F32)).astype(BF16)
    wrh = wrh_ref[...]
    logits = (jnp.dot(h2_hi, wrh, preferred_element_type=F32)
              + jnp.dot(h2_lo, wrh, preferred_element_type=F32)
              + jnp.dot(h2_hi, wrl_ref[...], preferred_element_type=F32)) + br_ref[...]
    lane = lax.broadcasted_iota(jnp.int32, logits.shape, 1)
    lane_f = lane.astype(F32)
    l = jnp.where(lane < n_exp, logits, NEG_INF)
    vals, idxs = [], []
    for _ in range(TOP_K):
        mx = jnp.max(l, axis=1, keepdims=True)
        idx = jnp.min(jnp.where(l == mx, lane_f, float(LANES)), axis=1, keepdims=True)
        vals.append(mx)
        idxs.append(idx)
        l = jnp.where(lane_f == idx, NEG_INF, l)
    es = [jnp.exp(v - vals[0]) for v in vals]
    den = es[0]
    for e in es[1:]:
        den = den + e
    ti = jnp.zeros(logits.shape, F32)
    tg = jnp.zeros(logits.shape, F32)
    for k in range(TOP_K):
        ti = jnp.where(lane == k, idxs[k], ti)
        tg = jnp.where(lane == k, es[k] / den, tg)
    ti_ref[0] = ti.astype(jnp.int32)
    tg_ref[0] = tg


def _merge(rnn, attn, u, x, g1, sc2, sh2, wao, wmo, gpm, gpf, wr_hi, wr_lo, br, n_exp, tm,
           gr_blk, ga_blk, h2_rows, h2_row0, h2_buf=None):
    g_, t, d = x.shape
    nt = t // tm
    assert h2_row0 % tm == 0
    kern = functools.partial(_merge_kernel, n_exp=n_exp)
    tok = lambda: pl.BlockSpec((1, tm, d), lambda g, i: (g, i, 0))
    full2 = lambda a: pl.BlockSpec(a.shape, lambda g, i: (0, 0))
    vec = lambda: pl.BlockSpec((1, d), lambda g, i: (0, 0))
    in_specs = [tok(), tok(),
                pl.BlockSpec((1, tm, d), lambda g, i: (g, i, gr_blk)),
                pl.BlockSpec((1, tm, d), lambda g, i: (g, i, ga_blk)),
                tok(), _mod_spec(g1, tm, 0), _mod_spec(sc2, tm, 0), _mod_spec(sh2, tm, 0),
                full2(wao), full2(wmo), vec(), vec(), full2(wr_hi), full2(wr_lo), full2(br)]
    args = [rnn, attn, u, u, x, g1, sc2, sh2, wao, wmo, gpm.reshape(1, d), gpf.reshape(1, d),
            wr_hi, wr_lo, br]
    aliases = {}
    if h2_buf is not None:
        in_specs.append(pl.BlockSpec(memory_space=pl.ANY))
        args.append(h2_buf)
        aliases = {len(args) - 1: 1}
    return pl.pallas_call(
        kern,
        grid=(g_, nt),
        in_specs=in_specs,
        out_specs=[tok(),
                   pl.BlockSpec((tm, d), lambda g, i: (h2_row0 // tm + g * nt + i, 0)),
                   pl.BlockSpec((1, tm, LANES), lambda g, i: (g, i, 0)),
                   pl.BlockSpec((1, tm, LANES), lambda g, i: (g, i, 0))],
        out_shape=[jax.ShapeDtypeStruct((g_, t, d), F32),
                   jax.ShapeDtypeStruct((h2_rows, d), BF16),
                   jax.ShapeDtypeStruct((g_, t, LANES), jnp.int32),
                   jax.ShapeDtypeStruct((g_, t, LANES), F32)],
        input_output_aliases=aliases,
        compiler_params=_cparams("arbitrary", "arbitrary"),
        name="merge_router",
    )(*args)


def _expert_kernel(te_ref, na_ref, x_ref, wg_ref, bg_ref, wu_ref, bu_ref, wd_ref, bd_ref, o_ref,
                   wg_scr, wu_scr, wd_scr):
    t = pl.program_id(0)
    e = te_ref[t]
    prev = te_ref[jnp.maximum(t - 1, 0)]
    active = t < na_ref[0]

    @pl.when(active & ((t == 0) | (e != prev)))
    def _():
        wg_scr[...] = wg_ref[0].astype(BF16)
        wu_scr[...] = wu_ref[0].astype(BF16)
        wd_scr[...] = wd_ref[0].astype(BF16)

    @pl.when(active)
    def _():
        x = x_ref[...]
        glu = jnp.minimum(jnp.dot(x, wg_scr[...], preferred_element_type=F32) + bg_ref[0],
                          SWIGLU_LIMIT)
        lin = jnp.clip(jnp.dot(x, wu_scr[...], preferred_element_type=F32) + bu_ref[0],
                       -SWIGLU_LIMIT, SWIGLU_LIMIT)
        hid = glu * jax.nn.sigmoid(SWIGLU_ALPHA * glu) * (lin + 1.0)
        out = jnp.dot(hid.astype(BF16), wd_scr[...], preferred_element_type=F32) + bd_ref[0]
        o_ref[...] = out.astype(BF16)

    @pl.when(jnp.logical_not(active))
    def _():
        o_ref[...] = jnp.zeros(o_ref.shape, o_ref.dtype)


def _experts(tile_expert, n_active, xs, wg, bg, wu, bu, wd, bd, tm):
    n_rows, d = xs.shape
    n_exp, _, dff = wg.shape
    w_spec = lambda a: pl.BlockSpec((1,) + a.shape[1:], lambda t, te, na: (te[t], 0, 0))
    grid_spec = pltpu.PrefetchScalarGridSpec(
        num_scalar_prefetch=2,
        grid=(n_rows // tm,),
        in_specs=[pl.BlockSpec((tm, d), lambda t, te, na: (t, 0)),
                  w_spec(wg), w_spec(bg), w_spec(wu), w_spec(bu), w_spec(wd), w_spec(bd)],
        out_specs=pl.BlockSpec((tm, d), lambda t, te, na: (t, 0)),
        scratch_shapes=[pltpu.VMEM((d, dff), BF16),
                        pltpu.VMEM((d, dff), BF16),
                        pltpu.VMEM((dff, d), BF16)],
    )
    return pl.pallas_call(
        _expert_kernel,
        grid_spec=grid_spec,
        out_shape=jax.ShapeDtypeStruct((n_rows, d), BF16),
        compiler_params=_cparams("arbitrary"),
        name="expert_ffn",
    )(tile_expert, n_active, xs, wg, bg, wu, bu, wd, bd)


def _combine_kernel(*refs):
    o_refs = refs[:TOP_K]
    tg_ref, x1_ref, g2_ref, gain_ref, out_ref = refs[TOP_K:]
    tg = tg_ref[0]
    y = None
    for k in range(TOP_K):
        term = tg[:, k:k + 1] * o_refs[k][0].astype(F32)
        y = term if y is None else y + term
    out_ref[0] = x1_ref[0] + g2_ref[0] * _rms(y, gain_ref[...])


def _combine(outs, tg, x1, g2, gain, tm):
    g_, t, d = x1.shape
    tok = lambda: pl.BlockSpec((1, tm, d), lambda g, i: (g, i, 0))
    return pl.pallas_call(
        _combine_kernel,
        grid=(g_, t // tm),
        in_specs=[tok() for _ in range(TOP_K)]
                 + [pl.BlockSpec((1, tm, LANES), lambda g, i: (g, i, 0)), tok(),
                    _mod_spec(g2, tm, 0), pl.BlockSpec((1, d), lambda g, i: (0, 0))],
        out_specs=tok(),
        out_shape=jax.ShapeDtypeStruct((g_, t, d), F32),
        compiler_params=_cparams("arbitrary", "arbitrary"),
        name="moe_combine",
    )(*outs, tg, x1, g2, gain.reshape(1, d))


def _route(top_i, n_exp, tm):
    n = top_i.shape[0]
    n_pairs = n * TOP_K
    eid = top_i.reshape(-1)
    experts = jnp.arange(n_exp, dtype=jnp.int32)
    onehot = (eid[:, None] == experts[None, :]).astype(jnp.int32)
    csum = jnp.cumsum(onehot, axis=0)
    counts = csum[-1]
    rank = jnp.sum(onehot * csum, axis=1) - 1
    padded = ((counts + tm - 1) // tm) * tm
    pad_end = jnp.cumsum(padded)
    dest = jnp.sum(onehot * (pad_end - padded)[None, :], axis=1) + rank
    n_tiles = -(-n_pairs // tm) + n_exp
    bits = n_pairs.bit_length()
    low = (1 << bits) - 1
    need_end = jnp.cumsum(padded - counts)
    dummy = jnp.arange(n_tiles * tm - n_pairs, dtype=jnp.int32)
    dummy_exp = jnp.sum((need_end[None, :] <= dummy[:, None]).astype(jnp.int32), axis=1)
    keys = jnp.concatenate([(eid << bits) | jnp.arange(n_pairs, dtype=jnp.int32),
                            (dummy_exp << bits) | low])
    pair = jnp.sort(keys) & low
    spread = jnp.arange(n_tiles * tm, dtype=jnp.int32) % n
    row_token = jnp.where(pair == low, spread, pair // TOP_K)
    tile_start = jnp.arange(n_tiles, dtype=jnp.int32) * tm
    tile_expert = jnp.sum((pad_end[None, :] <= tile_start[:, None]).astype(jnp.int32), axis=1)
    n_active = pad_end[-1] // tm
    last_e = jnp.max(jnp.where(counts > 0, experts, 0))
    tile_expert = jnp.where(tile_start < pad_end[-1], tile_expert, last_e)
    return dest.reshape(n, TOP_K), row_token, tile_expert, n_active.reshape(1)


def _rope_tables(pos):
    half = HEAD_DIM // 2
    inv = ROPE_THETA ** (-jnp.arange(half, dtype=F32) / half)
    ang = pos.astype(F32)[:, None] * inv[None, :]
    cos, sin = jnp.cos(ang), jnp.sin(ang)
    reps = LANES // HEAD_DIM
    cos_t = jnp.tile(cos, (1, 2 * reps))
    sin_t = jnp.tile(jnp.concatenate([-sin, sin], axis=1), (1, reps))
    return cos_t, sin_t


def _stream_kv():
    return (jnp.arange(2 * N_HEADS) % N_HEADS) // (N_HEADS // N_KV_HEADS)


def _decode_q(q_bf):
    db = q_bf.shape[0]
    q4 = q_bf.reshape(db, N_HEADS, 2, HEAD_DIM)
    eye_m = jnp.eye(2, dtype=q_bf.dtype)
    t = q4[:, None, :, :, :] * eye_m[None, :, None, :, None]
    t = t.reshape(db, 2 * N_HEADS, 2 * HEAD_DIM)
    sel = (_stream_kv()[None, :] == jnp.arange(N_KV_HEADS)[:, None]).astype(q_bf.dtype)
    return t[:, None, :, :] * sel[None, :, :, None]


def _gather_rows(a, idx):
    return a.at[idx].get(mode="promise_in_bounds")


def _layer(xp, xs, c_all, cache_k, cache_v, conv_state, h_state, page_table, past, p, lam_init):
    b, s, d = xp.shape
    db = xs.shape[0]
    d_rnn = p["conv_w"].shape[1]
    q_w = N_HEADS * 2 * HEAD_DIM
    k_w = N_KV_HEADS * 2 * HEAD_DIM
    v_w = N_KV_HEADS * V_DIM
    q_off = 2 * d_rnn
    k_off = q_off + q_w
    v_off = k_off + k_w
    gr_off = v_off + v_w
    ga_off = gr_off + d
    n_exp = p["w_router"].shape[1]
    tm = 512

    mod = _ada(c_all, p["w_ada"], p["b_ada"])
    mod_s = [m.reshape(1, db, d) for m in jnp.split(mod[:db], 6, axis=-1)]
    mod_p = [m.reshape(b, 1, d) for m in jnp.split(mod[db:db + b], 6, axis=-1)]

    w_in = p["w_in"].astype(BF16)
    wa = p["w_gate_a"].astype(BF16)
    wx = p["w_gate_x"].astype(BF16)
    w_rnn_out = p["w_rnn_out"].astype(BF16)
    w_attn_out = p["w_attn_out"].astype(BF16)
    w_mix_out = p["w_mix_out"].astype(BF16)
    wr = jnp.pad(p["w_router"], ((0, 0), (0, LANES - n_exp)))
    wr_hi = wr.astype(BF16)
    wr_lo = (wr - wr_hi.astype(F32)).astype(BF16)
    br = jnp.pad(p["b_router"], (0, LANES - n_exp)).reshape(1, LANES)
    lam = (jnp.exp(jnp.sum(p["lambda_q1"] * p["lambda_k1"]))
           - jnp.exp(jnp.sum(p["lambda_q2"] * p["lambda_k2"])) + lam_init).reshape(1).astype(F32)

    xs_g = xs.reshape(1, db, d)

    u_p = _inproj(xp, mod_p[1], mod_p[0], p["g_pre_mix"], w_in, tm)
    u_s = _inproj(xs_g, mod_s[1], mod_s[0], p["g_pre_mix"], w_in, db)

    cos_p, sin_p = _rope_tables(jnp.arange(s, dtype=jnp.int32))
    cos_s, sin_s = _rope_tables(jnp.full((db,), past, dtype=jnp.int32))
    qb_p, kf_p, kb_p, vf_p, vb_p = _prep(u_p, cos_p, sin_p, tm, q_w, k_w, v_w, q_off, k_off, v_off)
    qb_s, kf_s, kb_s, vf_s, vb_s = _prep(u_s, cos_s, sin_s, db, q_w, k_w, v_w, q_off, k_off, v_off)

    attn_p = _flash(lam, qb_p, kb_p, vb_p, p["g_subln"], lam_init, 256)
    qt = _decode_q(qb_s[0])
    k_new = kf_s.reshape(db, N_KV_HEADS, 2 * HEAD_DIM)[:, _stream_kv(), :]
    v_new = vf_s.reshape(db, N_KV_HEADS, V_DIM)[:, _stream_kv(), :]
    attn_s = _decode(page_table, lam, qt, k_new, v_new, p["g_subln"], cache_k, cache_v,
                     lam_init, 8).reshape(1, db, N_HEADS * V_DIM).astype(BF16)

    nw = p["conv_w"].shape[0]
    zero_conv = jnp.zeros((b, 8, d_rnn), F32)
    zero_h = jnp.zeros((b, 1, d_rnn), F32)
    rnn_p, conv_p8, h_p = _lru_scan(u_p, zero_conv, zero_h, p["conv_w"], p["conv_b"], wa,
                                    p["b_gate_a"], wx, p["b_gate_x"], p["lru_lambda"], w_rnn_out,
                                    256, d_rnn, True)
    conv_p = conv_p8[:, 8 - (nw - 1):, :]
    conv_state_t = jnp.transpose(conv_state, (1, 0, 2))
    rnn_s, h_s = _lru_step(u_s, conv_state_t, h_state, p["conv_w"], p["conv_b"], wa, p["b_gate_a"],
                           wx, p["b_gate_x"], p["lru_lambda"], w_rnn_out, d_rnn, past == 0)
    conv_s = jnp.concatenate([conv_state[:, 1:, :], u_s[0][:, None, :d_rnn]], axis=1)

    n_p = b * s
    x1_p, h2_all, ti_p, tg_p = _merge(rnn_p, attn_p, u_p, xp, mod_p[2], mod_p[4], mod_p[3],
                                      w_attn_out, w_mix_out, p["g_post_mix"], p["g_pre_ffn"],
                                      wr_hi, wr_lo, br, n_exp, tm, gr_off // d, ga_off // d,
                                      n_p + db, 0)
    x1_s, h2_all, ti_s, tg_s = _merge(rnn_s.reshape(1, db, d), attn_s, u_s, xs_g, mod_s[2],
                                      mod_s[4], mod_s[3], w_attn_out, w_mix_out, p["g_post_mix"],
                                      p["g_pre_ffn"], wr_hi, wr_lo, br, n_exp, db,
                                      gr_off // d, ga_off // d, n_p + db, n_p, h2_all)

    ti_all = jnp.concatenate([ti_p.reshape(n_p, LANES), ti_s.reshape(db, LANES)], axis=0)[:, :TOP_K]
    te_tm = 256
    dest, row_token, tile_expert, n_active = _route(ti_all, n_exp, te_tm)
    x_sorted = _gather_rows(h2_all, row_token)
    o_sorted = _experts(tile_expert, n_active, x_sorted, p["w_gate_e"],
                        p["b_gate_e"][:, None, :], p["w_up_e"], p["b_up_e"][:, None, :],
                        p["w_down_e"], p["b_down_e"][:, None, :], te_tm)
    outs_p = [_gather_rows(o_sorted, dest[:n_p, k]).reshape(b, s, d) for k in range(TOP_K)]
    outs_s = [_gather_rows(o_sorted, dest[n_p:, k]).reshape(1, db, d) for k in range(TOP_K)]
    y_p = _combine(outs_p, tg_p, x1_p, mod_p[5], p["g_post_ffn"], tm)
    y_s = _combine(outs_s, tg_s, x1_s, mod_s[5], p["g_post_ffn"], db)

    state = (kf_p.reshape(b, s, N_KV_HEADS, 2 * HEAD_DIM), vf_p.reshape(b, s, N_KV_HEADS, V_DIM),
             kf_s.reshape(db, 1, N_KV_HEADS, 2 * HEAD_DIM), vf_s.reshape(db, 1, N_KV_HEADS, V_DIM),
             conv_p, conv_s, h_p.reshape(b, d_rnn), h_s)
    return y_p, y_s.reshape(db, 1, d), state


def kernel(x_prompt, x_sample, cache_k, cache_v, state_conv, state_rglru, page_table, c_prompt, c_sample, w_ada, b_ada, g_pre_mix, g_post_mix, g_pre_ffn, g_post_ffn, w_in, conv_w, conv_b, w_gate_a, b_gate_a, w_gate_x, b_gate_x, lru_lambda, w_rnn_out, lambda_q1, lambda_k1, lambda_q2, lambda_k2, g_subln, w_attn_out, w_mix_out, w_router, b_router, w_gate_e, b_gate_e, w_up_e, b_up_e, w_down_e, b_down_e):
    depth = w_in.shape[0]
    db, dec_seq, d = x_sample.shape
    assert dec_seq == 1
    b = x_prompt.shape[0]
    n_pool, page = cache_k.shape[1], cache_k.shape[2]
    pad_rows = (-(db + b)) % 8
    c_all = jnp.concatenate([c_sample, c_prompt, jnp.zeros((pad_rows, d), F32)], axis=0)
    xp, xs = x_prompt, x_sample
    states = []
    params = dict(w_ada=w_ada, b_ada=b_ada, g_pre_mix=g_pre_mix, g_post_mix=g_post_mix,
                  g_pre_ffn=g_pre_ffn, g_post_ffn=g_post_ffn, w_in=w_in, conv_w=conv_w,
                  conv_b=conv_b, w_gate_a=w_gate_a, b_gate_a=b_gate_a, w_gate_x=w_gate_x,
                  b_gate_x=b_gate_x, lru_lambda=lru_lambda, w_rnn_out=w_rnn_out,
                  lambda_q1=lambda_q1, lambda_k1=lambda_k1, lambda_q2=lambda_q2,
                  lambda_k2=lambda_k2, g_subln=g_subln, w_attn_out=w_attn_out,
                  w_mix_out=w_mix_out, w_router=w_router, b_router=b_router,
                  w_gate_e=w_gate_e, b_gate_e=b_gate_e, w_up_e=w_up_e, b_up_e=b_up_e,
                  w_down_e=w_down_e, b_down_e=b_down_e)
    take = (lambda a, l: a.reshape(a.shape[1:])) if depth == 1 else (lambda a, l: a[l])
    for l in range(depth):
        p = {name: take(a, l) for name, a in params.items()}
        lam_init = 0.8 - 0.6 * math.exp(-0.3 * l)
        ck = take(cache_k, l).reshape(n_pool, page * N_KV_HEADS, -1)
        cv = take(cache_v, l).reshape(n_pool, page * N_KV_HEADS, -1)
        xp, xs_new, st = _layer(xp, xs.reshape(db, d), c_all, ck, cv, take(state_conv, l),
                                take(state_rglru, l), page_table, page_table.shape[1] * page,
                                p, lam_init)
        xs = xs_new
        states.append(st)
    stacked = [jnp.stack([st[i] for st in states]) for i in range(8)]
    return (xp, xs, *stacked)
```

```python
import functools
import math

import jax
import jax.numpy as jnp
from jax import lax
from jax.experimental import pallas as pl
from jax.experimental.pallas import tpu as pltpu

F32 = jnp.float32
BF16 = jnp.bfloat16

N_HEADS = 8
N_KV_HEADS = 4
HEAD_DIM = 64
V_DIM = 2 * HEAD_DIM
LRU_C = 8.0
ROPE_THETA = 10000.0
TOP_K = 4
SWIGLU_ALPHA = 1.702
SWIGLU_LIMIT = 7.0
EPS = 1e-6

LANES = 128
VMEM_LIMIT = 56 * 1024 * 1024
NEG_INF = float("-inf")


def _cparams(*sem):
    return pltpu.CompilerParams(dimension_semantics=sem, vmem_limit_bytes=VMEM_LIMIT)


def _rms(x, g):
    return x * lax.rsqrt(jnp.mean(x * x, axis=-1, keepdims=True) + EPS) * g


def _split(a):
    hi = a.astype(BF16)
    return hi, (a - hi.astype(F32)).astype(BF16)


def _mm(a, w):
    if w.dtype == BF16:
        return jnp.dot(a.astype(BF16), w, preferred_element_type=F32)
    a_hi, a_lo = _split(a.astype(F32))
    w_hi, w_lo = _split(w)
    return (jnp.dot(a_hi, w_hi, preferred_element_type=F32)
            + jnp.dot(a_lo, w_hi, preferred_element_type=F32)
            + jnp.dot(a_hi, w_lo, preferred_element_type=F32))


def _ada_kernel(c_ref, w_ref, b_ref, o_ref):
    c = c_ref[...]
    s = (c * jax.nn.sigmoid(c)).astype(BF16)
    o_ref[...] = jnp.dot(s, w_ref[...].astype(BF16), preferred_element_type=F32) + b_ref[...]


def _ada(c, w, b):
    m, d = c.shape
    n = w.shape[1]
    tn = 1024
    return pl.pallas_call(
        _ada_kernel,
        grid=(n // tn,),
        in_specs=[pl.BlockSpec((m, d), lambda j: (0, 0)),
                  pl.BlockSpec((d, tn), lambda j: (0, j)),
                  pl.BlockSpec((1, tn), lambda j: (0, j))],
        out_specs=pl.BlockSpec((m, tn), lambda j: (0, j)),
        out_shape=jax.ShapeDtypeStruct((m, n), F32),
        compiler_params=_cparams("arbitrary"),
        name="ada_mod",
    )(c, w, b.reshape(1, n))


def _inproj_kernel(x_ref, sc_ref, sh_ref, g_ref, w_ref, o_ref):
    h = _rms(x_ref[0], g_ref[...])
    h = h * (1.0 + sc_ref[0]) + sh_ref[0]
    o_ref[0] = _mm(h, w_ref[...])


def _mod_spec(mod, tm, n_lead):
    per_row = mod.shape[1] != 1
    d = mod.shape[2]
    rows = tm if per_row else 1
    if n_lead == 1:
        return pl.BlockSpec((1, rows, d), lambda n, g, i: (g, i if per_row else 0, 0))
    return pl.BlockSpec((1, rows, d), lambda g, i: (g, i if per_row else 0, 0))


def _inproj(x, sc, sh, gain, w_bf, tm):
    g_, t, d = x.shape
    n = w_bf.shape[1]
    tn = 2048
    return pl.pallas_call(
        _inproj_kernel,
        grid=(n // tn, g_, t // tm),
        in_specs=[pl.BlockSpec((1, tm, d), lambda n, g, i: (g, i, 0)),
                  _mod_spec(sc, tm, 1), _mod_spec(sh, tm, 1),
                  pl.BlockSpec((1, d), lambda n, g, i: (0, 0)),
                  pl.BlockSpec((d, tn), lambda n, g, i: (0, n))],
        out_specs=pl.BlockSpec((1, tm, tn), lambda n, g, i: (g, i, n)),
        out_shape=jax.ShapeDtypeStruct((g_, t, n), F32),
        compiler_params=_cparams("arbitrary", "arbitrary", "arbitrary"),
        name="in_proj",
    )(x, sc, sh, gain.reshape(1, d), w_bf)


def _rope_heads(x, c, s, first_half):
    outs = []
    for h in range(x.shape[1] // LANES):
        xh = x[:, h * LANES:(h + 1) * LANES]
        swapped = jnp.where(first_half, pltpu.roll(xh, LANES - HEAD_DIM // 2, 1),
                            pltpu.roll(xh, HEAD_DIM // 2, 1))
        outs.append(xh * c + swapped * s)
    return jnp.concatenate(outs, axis=1)


def _prep_kernel(q_ref, k_ref, v_ref, c_ref, s_ref, qb_ref, kf_ref, kb_ref, vf_ref, vb_ref):
    c = c_ref[...]
    s = s_ref[...]
    lane = lax.broadcasted_iota(jnp.int32, c.shape, 1)
    first_half = (lane & (HEAD_DIM - 1)) < HEAD_DIM // 2
    q = _rope_heads(q_ref[0], c, s, first_half) * (HEAD_DIM ** -0.5)
    k = _rope_heads(k_ref[0], c, s, first_half)
    v = v_ref[0]
    qb_ref[0] = q.astype(BF16)
    kb_ref[0] = k.astype(BF16)
    tm = k.shape[0]
    n_kv = k.shape[1] // LANES
    for j in range(n_kv):
        kf_ref[0, pl.ds(j, tm, stride=n_kv), :] = k[:, j * LANES:(j + 1) * LANES]
        vf_ref[0, pl.ds(j, tm, stride=n_kv), :] = v[:, j * LANES:(j + 1) * LANES]
    vb = v.astype(BF16)
    ones = jnp.ones((v.shape[0], V_DIM), BF16)
    parts = []
    for j in range(v.shape[1] // V_DIM):
        parts += [vb[:, j * V_DIM:(j + 1) * V_DIM], ones]
    vb_ref[0] = jnp.concatenate(parts, axis=1)


def _prep(u, cos_t, sin_t, tm, q_w, k_w, v_w, q_off, k_off, v_off):
    g_, t, _ = u.shape
    assert k_w == v_w and k_w % LANES == 0
    n_kv = k_w // LANES
    return pl.pallas_call(
        _prep_kernel,
        grid=(g_, t // tm),
        in_specs=[pl.BlockSpec((1, tm, q_w), lambda g, i: (g, i, q_off // q_w)),
                  pl.BlockSpec((1, tm, k_w), lambda g, i: (g, i, k_off // k_w)),
                  pl.BlockSpec((1, tm, v_w), lambda g, i: (g, i, v_off // v_w)),
                  pl.BlockSpec((tm, LANES), lambda g, i: (i, 0)),
                  pl.BlockSpec((tm, LANES), lambda g, i: (i, 0))],
        out_specs=[pl.BlockSpec((1, tm, q_w), lambda g, i: (g, i, 0)),
                   pl.BlockSpec((1, tm * n_kv, LANES), lambda g, i: (g, i, 0)),
                   pl.BlockSpec((1, tm, k_w), lambda g, i: (g, i, 0)),
                   pl.BlockSpec((1, tm * n_kv, LANES), lambda g, i: (g, i, 0)),
                   pl.BlockSpec((1, tm, 2 * v_w), lambda g, i: (g, i, 0))],
        out_shape=[jax.ShapeDtypeStruct((g_, t, q_w), BF16),
                   jax.ShapeDtypeStruct((g_, t * n_kv, LANES), F32),
                   jax.ShapeDtypeStruct((g_, t, k_w), BF16),
                   jax.ShapeDtypeStruct((g_, t * n_kv, LANES), F32),
                   jax.ShapeDtypeStruct((g_, t, 2 * v_w), BF16)],
        compiler_params=_cparams("arbitrary", "arbitrary"),
        name="qkv_prep",
    )(u, u, u, cos_t, sin_t)


def _flash_kernel(lam_ref, q_ref, k_ref, v_ref, gs_ref, o_ref, q4_scr, m_scr, acc_scr,
                  *, tq, lam_init):
    i = pl.program_id(2)
    n_rep = N_HEADS // N_KV_HEADS
    q = q_ref[0]
    lane = lax.broadcasted_iota(jnp.int32, (tq, LANES), 1)
    is_map1 = lane < HEAD_DIM
    zero = jnp.zeros((tq, LANES), BF16)
    for g in range(n_rep):
        qg = q[:, g * LANES:(g + 1) * LANES]
        q4_scr[(2 * g) * tq:(2 * g + 1) * tq, :] = jnp.where(is_map1, qg, zero)
        q4_scr[(2 * g + 1) * tq:(2 * g + 2) * tq, :] = jnp.where(is_map1, zero, qg)
    m_scr[...] = jnp.full(m_scr.shape, NEG_INF, F32)
    acc_scr[...] = jnp.zeros(acc_scr.shape, F32)
    rows = 2 * n_rep * tq

    def step(start, size, masked):
        k = k_ref[0, pl.ds(start, size), :]
        v = v_ref[0, pl.ds(start, size), :]
        for r0 in range(0, rows, tq):
            rs = slice(r0, r0 + tq)
            s = lax.dot_general(q4_scr[rs, :], k, (((1,), (1,)), ((), ())),
                                preferred_element_type=F32)
            if masked:
                r = lax.broadcasted_iota(jnp.int32, (tq, size), 0)
                c = lax.broadcasted_iota(jnp.int32, (tq, size), 1)
                s = jnp.where(r >= c, s, NEG_INF)
            m_prev = m_scr[rs, :]
            m_new = jnp.maximum(m_prev, jnp.max(s, axis=1, keepdims=True))
            alpha = jnp.exp(m_prev - m_new)
            p = jnp.exp(s - jnp.concatenate([m_new] * (size // LANES), axis=1))
            acc_scr[rs, :] = (jnp.concatenate([alpha, alpha], axis=1) * acc_scr[rs, :]
                              + jnp.dot(p.astype(BF16), v, preferred_element_type=F32))
            m_scr[rs, :] = m_new

    def body(t, carry):
        step(pl.multiple_of(t * 4 * tq, 4 * tq), 4 * tq, False)
        return carry

    lax.fori_loop(0, i // 4, body, 0)

    @pl.when((i & 2) != 0)
    def _():
        step(pl.multiple_of((i // 4) * 4 * tq, 2 * tq), 2 * tq, False)

    @pl.when((i & 1) != 0)
    def _():
        step(pl.multiple_of((i - 1) * tq, tq), tq, False)

    step(pl.multiple_of(i * tq, tq), tq, True)

    o = acc_scr[:, 0:V_DIM] / acc_scr[:, V_DIM:2 * V_DIM]
    lam = lam_ref[0]
    for g in range(n_rep):
        o1 = o[(2 * g) * tq:(2 * g + 1) * tq, :]
        o2 = o[(2 * g + 1) * tq:(2 * g + 2) * tq, :]
        d = _rms(o1 - lam * o2, gs_ref[...]) * (1.0 - lam_init)
        o_ref[0, :, g * LANES:(g + 1) * LANES] = d.astype(BF16)


def _flash(lam, q_bf, k_bf, v_bf, g_subln, lam_init, tq):
    b, s, _ = q_bf.shape
    n_rep = N_HEADS // N_KV_HEADS
    qw = n_rep * 2 * HEAD_DIM
    rows = 2 * n_rep * tq
    kern = functools.partial(_flash_kernel, tq=tq, lam_init=lam_init)
    return pl.pallas_call(
        kern,
        grid=(b, N_KV_HEADS, s // tq),
        in_specs=[pl.BlockSpec(memory_space=pltpu.SMEM),
                  pl.BlockSpec((1, tq, qw), lambda b_, j, i: (b_, i, j)),
                  pl.BlockSpec((1, s, 2 * HEAD_DIM), lambda b_, j, i: (b_, 0, j)),
                  pl.BlockSpec((1, s, 2 * V_DIM), lambda b_, j, i: (b_, 0, j)),
                  pl.BlockSpec((1, V_DIM), lambda b_, j, i: (0, 0))],
        out_specs=pl.BlockSpec((1, tq, n_rep * V_DIM), lambda b_, j, i: (b_, i, j)),
        out_shape=jax.ShapeDtypeStruct((b, s, N_HEADS * V_DIM), BF16),
        scratch_shapes=[pltpu.VMEM((rows, LANES), BF16),
                        pltpu.VMEM((rows, LANES), F32),
                        pltpu.VMEM((rows, 2 * V_DIM), F32)],
        compiler_params=_cparams("arbitrary", "arbitrary", "arbitrary"),
        name="prompt_attn",
    )(lam, q_bf, k_bf, v_bf, g_subln.reshape(1, V_DIM))


def _decode_kernel(pt_ref, lam_ref, qt_ref, kn_ref, vn_ref, gs_ref, *rest, pps, lam_init):
    k_refs = rest[:pps]
    v_refs = rest[pps:2 * pps]
    o_ref = rest[2 * pps]
    m_scr, l_scr, acc_scr = rest[2 * pps + 1:]
    s_id = pl.program_id(1)
    n_steps = pl.num_programs(1)

    @pl.when(s_id == 0)
    def _():
        m_scr[...] = jnp.full(m_scr.shape, NEG_INF, F32)
        l_scr[...] = jnp.zeros(l_scr.shape, F32)
        acc_scr[...] = jnp.zeros(acc_scr.shape, F32)

    n_rep = N_HEADS // N_KV_HEADS
    n_str = 2 * N_HEADS
    page = k_refs[0].shape[1] // N_KV_HEADS
    row_kv = (lax.broadcasted_iota(jnp.int32, (n_str, LANES), 0) & (N_HEADS - 1)) // n_rep

    def kv_rows(refs, j):
        return jnp.concatenate([r[0, pl.ds(j, page, stride=N_KV_HEADS), :] for r in refs],
                               axis=0).astype(BF16)

    s = None
    for j in range(N_KV_HEADS):
        sj = lax.dot_general(qt_ref[0, j], kv_rows(k_refs, j), (((1,), (1,)), ((), ())),
                             preferred_element_type=F32)
        s = sj if s is None else s + sj
    m_prev = m_scr[...]
    m_new = jnp.maximum(m_prev, jnp.max(s, axis=1, keepdims=True))
    alpha = jnp.exp(m_prev - m_new)
    p = jnp.exp(s - jnp.concatenate([m_new] * (s.shape[1] // LANES), axis=1))
    l_scr[...] = alpha * l_scr[...] + jnp.sum(p, axis=1, keepdims=True)
    pb = p.astype(BF16)
    acc = alpha * acc_scr[...]
    for j in range(N_KV_HEADS):
        oj = jnp.dot(pb, kv_rows(v_refs, j), preferred_element_type=F32)
        acc = acc + jnp.where(row_kv == j, oj, 0.0)
    acc_scr[...] = acc
    m_scr[...] = m_new

    @pl.when(s_id == n_steps - 1)
    def _():
        qt = qt_ref[0, 0].astype(F32)
        for j in range(1, N_KV_HEADS):
            qt = qt + qt_ref[0, j].astype(F32)
        s_self = jnp.sum(qt * kn_ref[0], axis=1, keepdims=True)
        m_last = m_scr[...]
        m_fin = jnp.maximum(m_last, s_self)
        a_fin = jnp.exp(m_last - m_fin)
        p_self = jnp.exp(s_self - m_fin)
        l_fin = a_fin * l_scr[...] + p_self
        o = (a_fin * acc_scr[...] + p_self * vn_ref[0]) / l_fin
        d = o[0:N_HEADS, :] - lam_ref[0] * o[N_HEADS:n_str, :]
        o_ref[0] = _rms(d, gs_ref[...]) * (1.0 - lam_init)


def _decode(page_table, lam, qmat, k_new, v_new, g_subln, cache_k, cache_v, lam_init, pps):
    db, n_pages = page_table.shape
    assert n_pages % pps == 0
    n_pool, page_rows, kw = cache_k.shape
    vw = cache_v.shape[2]
    n_str = 2 * N_HEADS
    pt_flat = page_table.reshape(-1)

    def page_spec(width, i):
        return pl.BlockSpec((1, page_rows, width),
                            lambda b, s, pt: (pt[b * n_pages + s * pps + i], 0, 0))

    kern = functools.partial(_decode_kernel, pps=pps, lam_init=lam_init)
    grid_spec = pltpu.PrefetchScalarGridSpec(
        num_scalar_prefetch=1,
        grid=(db, n_pages // pps),
        in_specs=[pl.BlockSpec(memory_space=pltpu.SMEM),
                  pl.BlockSpec((1, N_KV_HEADS, n_str, kw), lambda b, s, pt: (b, 0, 0, 0)),
                  pl.BlockSpec((1, n_str, kw), lambda b, s, pt: (b, 0, 0)),
                  pl.BlockSpec((1, n_str, vw), lambda b, s, pt: (b, 0, 0)),
                  pl.BlockSpec((1, V_DIM), lambda b, s, pt: (0, 0))]
                 + [page_spec(kw, i) for i in range(pps)]
                 + [page_spec(vw, i) for i in range(pps)],
        out_specs=pl.BlockSpec((1, N_HEADS, V_DIM), lambda b, s, pt: (b, 0, 0)),
        scratch_shapes=[pltpu.VMEM((n_str, LANES), F32),
                        pltpu.VMEM((n_str, LANES), F32),
                        pltpu.VMEM((n_str, vw), F32)],
    )
    return pl.pallas_call(
        kern,
        grid_spec=grid_spec,
        out_shape=jax.ShapeDtypeStruct((db, N_HEADS, V_DIM), F32),
        compiler_params=_cparams("arbitrary", "arbitrary"),
        name="decode_attn",
    )(pt_flat, lam, qmat, k_new, v_new, g_subln.reshape(1, V_DIM),
      *([cache_k] * pps), *([cache_v] * pps))


def _gelu_tanh(x):
    return 0.5 * x * (1.0 + jnp.tanh(math.sqrt(2.0 / math.pi) * (x + 0.044715 * (x * x * x))))


def _softplus(z):
    return jnp.maximum(z, 0.0) + jnp.log1p(jnp.exp(-jnp.abs(z)))


def _block_diag(y, w_ref, b_ref):
    nb = w_ref.shape[0]
    parts = [_mm(y[:, n * LANES:(n + 1) * LANES], w_ref[n]) for n in range(nb)]
    return jnp.concatenate(parts, axis=1) + b_ref[...]


def _lru_gates(y, wa_ref, ba_ref, wx_ref, bx_ref, lam_ref, reset):
    if wa_ref.dtype == BF16:
        y_in = y.astype(BF16)
    else:
        y_in = y
    r = jax.nn.sigmoid(_block_diag(y_in, wa_ref, ba_ref))
    i = jax.nn.sigmoid(_block_diag(y_in, wx_ref, bx_ref))
    log_a = -LRU_C * r * _softplus(-lam_ref[...])
    a = jnp.exp(log_a)
    mult = jnp.sqrt(-jnp.tanh(log_a) * (a * a + 1.0))
    if reset is not None:
        a = jnp.where(reset, 0.0, a)
        mult = jnp.where(reset, 1.0, mult)
    return a, mult * i * y


def _scan_kernel(xr_ref, yr_ref, cs_ref, h0_ref, cw_ref, cb_ref, wa_ref, ba_ref, wx_ref, bx_ref,
                 lam_ref, wo_ref, o_ref, cn_ref, hl_ref, xp_scr, a_scr, b_scr, h_scr, hc_scr,
                 *, tc, reset_first):
    c = pl.program_id(1)

    @pl.when(c == 0)
    def _():
        xp_scr[0:8, :] = cs_ref[0]
        hc_scr[...] = jnp.broadcast_to(h0_ref[0], hc_scr.shape)

    xr = xr_ref[0]
    xp_scr[8:8 + tc, :] = xr
    cw = cw_ref[...]
    nw = cw.shape[0]
    y = cb_ref[...] + cw[nw - 1:nw, :] * xr
    for j in range(nw - 1):
        off = 8 - (nw - 1) + j
        y = y + cw[j:j + 1, :] * xp_scr[off:off + tc, :]
    tail = xp_scr[tc:tc + 8, :]
    xp_scr[0:8, :] = tail
    cn_ref[0] = tail

    reset = None
    if reset_first:
        row = lax.broadcasted_iota(jnp.int32, xr.shape, 0)
        reset = (row + c * tc) == 0
    a, bx = _lru_gates(y, wa_ref, ba_ref, wx_ref, bx_ref, lam_ref, reset)
    a_scr[...] = a
    b_scr[...] = bx

    def body(t, h):
        h = a_scr[pl.ds(t, 1), :] * h + b_scr[pl.ds(t, 1), :]
        h_scr[pl.ds(t, 1), :] = h
        return h

    h_last = lax.fori_loop(0, tc, body, hc_scr[0:1, :], unroll=8)
    hc_scr[...] = jnp.broadcast_to(h_last, hc_scr.shape)
    hl_ref[0] = h_last
    gated = h_scr[...] * _gelu_tanh(yr_ref[0])
    o_ref[0] = _mm(gated, wo_ref[...]).astype(o_ref.dtype)


def _lru_scan(u, conv_state8, h0, conv_w, conv_b, wa, ba, wx, bx, lam, wo, tc, d, reset_first):
    g_, t, _ = u.shape
    nb = wa.shape[0]
    kern = functools.partial(_scan_kernel, tc=tc, reset_first=reset_first)
    vec = lambda: pl.BlockSpec((1, d), lambda g, c: (0, 0))
    return pl.pallas_call(
        kern,
        grid=(g_, t // tc),
        in_specs=[pl.BlockSpec((1, tc, d), lambda g, c: (g, c, 0)),
                  pl.BlockSpec((1, tc, d), lambda g, c: (g, c, 1)),
                  pl.BlockSpec((1, 8, d), lambda g, c: (g, 0, 0)),
                  pl.BlockSpec((1, 1, d), lambda g, c: (g, 0, 0)),
                  pl.BlockSpec(conv_w.shape, lambda g, c: (0, 0)),
                  vec(),
                  pl.BlockSpec(wa.shape, lambda g, c: (0, 0, 0)), vec(),
                  pl.BlockSpec(wx.shape, lambda g, c: (0, 0, 0)), vec(),
                  vec(),
                  pl.BlockSpec(wo.shape, lambda g, c: (0, 0))],
        out_specs=[pl.BlockSpec((1, tc, d), lambda g, c: (g, c, 0)),
                   pl.BlockSpec((1, 8, d), lambda g, c: (g, 0, 0)),
                   pl.BlockSpec((1, 1, d), lambda g, c: (g, 0, 0))],
        out_shape=[jax.ShapeDtypeStruct((g_, t, d), BF16),
                   jax.ShapeDtypeStruct((g_, 8, d), F32),
                   jax.ShapeDtypeStruct((g_, 1, d), F32)],
        scratch_shapes=[pltpu.VMEM((tc + 8, d), F32),
                        pltpu.VMEM((tc, d), F32),
                        pltpu.VMEM((tc, d), F32),
                        pltpu.VMEM((tc, d), F32),
                        pltpu.VMEM((8, d), F32)],
        compiler_params=_cparams("arbitrary", "arbitrary"),
        name="lru_scan",
    )(u, u, conv_state8, h0, conv_w, conv_b.reshape(1, d), wa, ba.reshape(1, d), wx,
      bx.reshape(1, d), lam.reshape(1, d), wo)


def _lru_step_kernel(xr_ref, yr_ref, s_ref, h0_ref, cw_ref, cb_ref, wa_ref, ba_ref, wx_ref, bx_ref,
                     lam_ref, wo_ref, o_ref, hn_ref, *, reset):
    xr = xr_ref[0]
    cw = cw_ref[...]
    nw = cw.shape[0]
    y = cb_ref[...] + cw[nw - 1:nw, :] * xr
    for j in range(nw - 1):
        y = y + cw[j:j + 1, :] * s_ref[j]
    a, bx = _lru_gates(y, wa_ref, ba_ref, wx_ref, bx_ref, lam_ref,
                       jnp.full(y.shape, True) if reset else None)
    h = a * h0_ref[...] + bx
    hn_ref[...] = h
    gated = h * _gelu_tanh(yr_ref[0])
    o_ref[...] = _mm(gated, wo_ref[...]).astype(o_ref.dtype)


def _lru_step(u, conv_state_t, h0, conv_w, conv_b, wa, ba, wx, bx, lam, wo, d, reset):
    m = u.shape[1]
    kern = functools.partial(_lru_step_kernel, reset=reset)
    vec = lambda: pl.BlockSpec((1, d), lambda i: (0, 0))
    return pl.pallas_call(
        kern,
        grid=(1,),
        in_specs=[pl.BlockSpec((1, m, d), lambda i: (0, 0, 0)),
                  pl.BlockSpec((1, m, d), lambda i: (0, 0, 1)),
                  pl.BlockSpec(conv_state_t.shape, lambda i: (0, 0, 0)),
                  pl.BlockSpec((m, d), lambda i: (0, 0)),
                  pl.BlockSpec(conv_w.shape, lambda i: (0, 0)),
                  vec(),
                  pl.BlockSpec(wa.shape, lambda i: (0, 0, 0)), vec(),
                  pl.BlockSpec(wx.shape, lambda i: (0, 0, 0)), vec(),
                  vec(),
                  pl.BlockSpec(wo.shape, lambda i: (0, 0))],
        out_specs=[pl.BlockSpec((m, d), lambda i: (0, 0)),
                   pl.BlockSpec((m, d), lambda i: (0, 0))],
        out_shape=[jax.ShapeDtypeStruct((m, d), wo.dtype),
                   jax.ShapeDtypeStruct((m, d), F32)],
        compiler_params=_cparams("arbitrary"),
        name="lru_step",
    )(u, u, conv_state_t, h0, conv_w, conv_b.reshape(1, d), wa, ba.reshape(1, d), wx,
      bx.reshape(1, d), lam.reshape(1, d), wo)


def _merge_kernel(rnn_ref, attn_ref, gr_ref, ga_ref, x_ref, g1_ref, sc2_ref, sh2_ref,
                  wao_ref, wmo_ref, gpm_ref, gpf_ref, wrh_ref, wrl_ref, br_ref, *rest, n_exp):
    x1_ref, h2_ref, ti_ref, tg_ref = rest[-4:]
    attn_p = _mm(attn_ref[0], wao_ref[...])
    merged = (jax.nn.sigmoid(gr_ref[0]) * rnn_ref[0].astype(F32)
              + jax.nn.sigmoid(ga_ref[0]) * attn_p)
    mix = _mm(merged, wmo_ref[...])
    x1 = x_ref[0] + g1_ref[0] * _rms(mix, gpm_ref[...])
    x1_ref[0] = x1
    h2 = _rms(x1, gpf_ref[...]) * (1.0 + sc2_ref[0]) + sh2_ref[0]
    h2_hi = h2.astype(BF16)
    h2_ref[...] = h2_hi
    h2_lo = (h2 - h2_hi.astype(F32)).astype(BF16)
    wrh = wrh_ref[...]
    logits = (jnp.dot(h2_hi, wrh, preferred_element_type=F32)
              + jnp.dot(h2_lo, wrh, preferred_element_type=F32)
              + jnp.dot(h2_hi, wrl_ref[...], preferred_element_type=F32)) + br_ref[...]
    lane = lax.broadcasted_iota(jnp.int32, logits.shape, 1)
    lane_f = lane.astype(F32)
    l = jnp.where(lane < n_exp, logits, NEG_INF)
    vals, idxs = [], []
    for _ in range(TOP_K):
        mx = jnp.max(l, axis=1, keepdims=True)
        idx = jnp.min(jnp.where(l == mx, lane_f, float(LANES)), axis=1, keepdims=True)
        vals.append(mx)
        idxs.append(idx)
        l = jnp.where(lane_f == idx, NEG_INF, l)
    es = [jnp.exp(v - vals[0]) for v in vals]
    den = es[0]
    for e in es[1:]:
        den = den + e
    ti = jnp.zeros(logits.shape, F32)
    tg = jnp.zeros(logits.shape, F32)
    for k in range(TOP_K):
        ti = jnp.where(lane == k, idxs[k], ti)
        tg = jnp.where(lane == k, es[k] / den, tg)
    ti_ref[0] = ti.astype(jnp.int32)
    tg_ref[0] = tg


def _merge(rnn, attn, u, x, g1, sc2, sh2, wao, wmo, gpm, gpf, wr_hi, wr_lo, br, n_exp, tm,
           gr_blk, ga_blk, h2_rows, h2_row0, h2_buf=None):
    g_, t, d = x.shape
    nt = t // tm
    assert h2_row0 % tm == 0
    kern = functools.partial(_merge_kernel, n_exp=n_exp)
    tok = lambda: pl.BlockSpec((1, tm, d), lambda g, i: (g, i, 0))
    full2 = lambda a: pl.BlockSpec(a.shape, lambda g, i: (0, 0))
    vec = lambda: pl.BlockSpec((1, d), lambda g, i: (0, 0))
    in_specs = [tok(), tok(),
                pl.BlockSpec((1, tm, d), lambda g, i: (g, i, gr_blk)),
                pl.BlockSpec((1, tm, d), lambda g, i: (g, i, ga_blk)),
                tok(), _mod_spec(g1, tm, 0), _mod_spec(sc2, tm, 0), _mod_spec(sh2, tm, 0),
                full2(wao), full2(wmo), vec(), vec(), full2(wr_hi), full2(wr_lo), full2(br)]
    args = [rnn, attn, u, u, x, g1, sc2, sh2, wao, wmo, gpm.reshape(1, d), gpf.reshape(1, d),
            wr_hi, wr_lo, br]
    aliases = {}
    if h2_buf is not None:
        in_specs.append(pl.BlockSpec(memory_space=pl.ANY))
        args.append(h2_buf)
        aliases = {len(args) - 1: 1}
    return pl.pallas_call(
        kern,
        grid=(g_, nt),
        in_specs=in_specs,
        out_specs=[tok(),
                   pl.BlockSpec((tm, d), lambda g, i: (h2_row0 // tm + g * nt + i, 0)),
                   pl.BlockSpec((1, tm, LANES), lambda g, i: (g, i, 0)),
                   pl.BlockSpec((1, tm, LANES), lambda g, i: (g, i, 0))],
        out_shape=[jax.ShapeDtypeStruct((g_, t, d), F32),
                   jax.ShapeDtypeStruct((h2_rows, d), BF16),
                   jax.ShapeDtypeStruct((g_, t, LANES), jnp.int32),
                   jax.ShapeDtypeStruct((g_, t, LANES), F32)],
        input_output_aliases=aliases,
        compiler_params=_cparams("arbitrary", "arbitrary"),
        name="merge_router",
    )(*args)


def _expert_kernel(te_ref, na_ref, x_ref, wg_ref, bg_ref, wu_ref, bu_ref, wd_ref, bd_ref, o_ref,
                   wg_scr, wu_scr, wd_scr):
    t = pl.program_id(0)
    e = te_ref[t]
    prev = te_ref[jnp.maximum(t - 1, 0)]
    active = t < na_ref[0]

    @pl.when(active & ((t == 0) | (e != prev)))
    def _():
        wg_scr[...] = wg_ref[0].astype(BF16)
        wu_scr[...] = wu_ref[0].astype(BF16)
        wd_scr[...] = wd_ref[0].astype(BF16)

    @pl.when(active)
    def _():
        x = x_ref[...]
        glu = jnp.minimum(jnp.dot(x, wg_scr[...], preferred_element_type=F32) + bg_ref[0],
                          SWIGLU_LIMIT)
        lin = jnp.clip(jnp.dot(x, wu_scr[...], preferred_element_type=F32) + bu_ref[0],
                       -SWIGLU_LIMIT, SWIGLU_LIMIT)
        hid = glu * jax.nn.sigmoid(SWIGLU_ALPHA * glu) * (lin + 1.0)
        out = jnp.dot(hid.astype(BF16), wd_scr[...], preferred_element_type=F32) + bd_ref[0]
        o_ref[...] = out.astype(BF16)

    @pl.when(jnp.logical_not(active))
    def _():
        o_ref[...] = jnp.zeros(o_ref.shape, o_ref.dtype)


def _experts(tile_expert, n_active, xs, wg, bg, wu, bu, wd, bd, tm):
    n_rows, d = xs.shape
    n_exp, _, dff = wg.shape
    w_spec = lambda a: pl.BlockSpec((1,) + a.shape[1:], lambda t, te, na: (te[t], 0, 0))
    grid_spec = pltpu.PrefetchScalarGridSpec(
        num_scalar_prefetch=2,
        grid=(n_rows // tm,),
        in_specs=[pl.BlockSpec((tm, d), lambda t, te, na: (t, 0)),
                  w_spec(wg), w_spec(bg), w_spec(wu), w_spec(bu), w_spec(wd), w_spec(bd)],
        out_specs=pl.BlockSpec((tm, d), lambda t, te, na: (t, 0)),
        scratch_shapes=[pltpu.VMEM((d, dff), BF16),
                        pltpu.VMEM((d, dff), BF16),
                        pltpu.VMEM((dff, d), BF16)],
    )
    return pl.pallas_call(
        _expert_kernel,
        grid_spec=grid_spec,
        out_shape=jax.ShapeDtypeStruct((n_rows, d), BF16),
        compiler_params=_cparams("arbitrary"),
        name="expert_ffn",
    )(tile_expert, n_active, xs, wg, bg, wu, bu, wd, bd)


def _combine_kernel(*refs):
    o_refs = refs[:TOP_K]
    tg_ref, x1_ref, g2_ref, gain_ref, out_ref = refs[TOP_K:]
    tg = tg_ref[0]
    y = None
    for k in range(TOP_K):
        term = tg[:, k:k + 1] * o_refs[k][0].astype(F32)
        y = term if y is None else y + term
    out_ref[0] = x1_ref[0] + g2_ref[0] * _rms(y, gain_ref[...])


def _combine(outs, tg, x1, g2, gain, tm):
    g_, t, d = x1.shape
    tok = lambda: pl.BlockSpec((1, tm, d), lambda g, i: (g, i, 0))
    return pl.pallas_call(
        _combine_kernel,
        grid=(g_, t // tm),
        in_specs=[tok() for _ in range(TOP_K)]
                 + [pl.BlockSpec((1, tm, LANES), lambda g, i: (g, i, 0)), tok(),
                    _mod_spec(g2, tm, 0), pl.BlockSpec((1, d), lambda g, i: (0, 0))],
        out_specs=tok(),
        out_shape=jax.ShapeDtypeStruct((g_, t, d), F32),
        compiler_params=_cparams("arbitrary", "arbitrary"),
        name="moe_combine",
    )(*outs, tg, x1, g2, gain.reshape(1, d))


def _route(top_i, n_exp, tm):
    n = top_i.shape[0]
    n_pairs = n * TOP_K
    eid = top_i.reshape(-1)
    experts = jnp.arange(n_exp, dtype=jnp.int32)
    onehot = (eid[:, None] == experts[None, :]).astype(jnp.int32)
    csum = jnp.cumsum(onehot, axis=0)
    counts = csum[-1]
    rank = jnp.sum(onehot * csum, axis=1) - 1
    padded = ((counts + tm - 1) // tm) * tm
    pad_end = jnp.cumsum(padded)
    dest = jnp.sum(onehot * (pad_end - padded)[None, :], axis=1) + rank
    n_tiles = -(-n_pairs // tm) + n_exp
    bits = n_pairs.bit_length()
    low = (1 << bits) - 1
    need_end = jnp.cumsum(padded - counts)
    dummy = jnp.arange(n_tiles * tm - n_pairs, dtype=jnp.int32)
    dummy_exp = jnp.sum((need_end[None, :] <= dummy[:, None]).astype(jnp.int32), axis=1)
    keys = jnp.concatenate([(eid << bits) | jnp.arange(n_pairs, dtype=jnp.int32),
                            (dummy_exp << bits) | low])
    pair = jnp.sort(keys) & low
    spread = jnp.arange(n_tiles * tm, dtype=jnp.int32) % n
    row_token = jnp.where(pair == low, spread, pair // TOP_K)
    tile_start = jnp.arange(n_tiles, dtype=jnp.int32) * tm
    tile_expert = jnp.sum((pad_end[None, :] <= tile_start[:, None]).astype(jnp.int32), axis=1)
    n_active = pad_end[-1] // tm
    last_e = jnp.max(jnp.where(counts > 0, experts, 0))
    tile_expert = jnp.where(tile_start < pad_end[-1], tile_expert, last_e)
    return dest.reshape(n, TOP_K), row_token, tile_expert, n_active.reshape(1)


def _rope_tables(pos):
    half = HEAD_DIM // 2
    inv = ROPE_THETA ** (-jnp.arange(half, dtype=F32) / half)
    ang = pos.astype(F32)[:, None] * inv[None, :]
    cos, sin = jnp.cos(ang), jnp.sin(ang)
    reps = LANES // HEAD_DIM
    cos_t = jnp.tile(cos, (1, 2 * reps))
    sin_t = jnp.tile(jnp.concatenate([-sin, sin], axis=1), (1, reps))
    return cos_t, sin_t


def _stream_kv():
    return (jnp.arange(2 * N_HEADS) % N_HEADS) // (N_HEADS // N_KV_HEADS)


def _decode_q(q_bf):
    db = q_bf.shape[0]
    q4 = q_bf.reshape(db, N_HEADS, 2, HEAD_DIM)
    eye_m = jnp.eye(2, dtype=q_bf.dtype)
    t = q4[:, None, :, :, :] * eye_m[None, :, None, :, None]
    t = t.reshape(db, 2 * N_HEADS, 2 * HEAD_DIM)
    sel = (_stream_kv()[None, :] == jnp.arange(N_KV_HEADS)[:, None]).astype(q_bf.dtype)
    return t[:, None, :, :] * sel[None, :, :, None]


def _gather_rows(a, idx):
    return a.at[idx].get(mode="promise_in_bounds")


def _layer(xp, xs, c_all, cache_k, cache_v, conv_state, h_state, page_table, past, p, lam_init):
    b, s, d = xp.shape
    db = xs.shape[0]
    d_rnn = p["conv_w"].shape[1]
    q_w = N_HEADS * 2 * HEAD_DIM
    k_w = N_KV_HEADS * 2 * HEAD_DIM
    v_w = N_KV_HEADS * V_DIM
    q_off = 2 * d_rnn
    k_off = q_off + q_w
    v_off = k_off + k_w
    gr_off = v_off + v_w
    ga_off = gr_off + d
    n_exp = p["w_router"].shape[1]
    tm = 512

    mod = _ada(c_all, p["w_ada"], p["b_ada"])
    mod_s = [m.reshape(1, db, d) for m in jnp.split(mod[:db], 6, axis=-1)]
    mod_p = [m.reshape(b, 1, d) for m in jnp.split(mod[db:db + b], 6, axis=-1)]

    w_in = p["w_in"].astype(BF16)
    wa = p["w_gate_a"].astype(BF16)
    wx = p["w_gate_x"].astype(BF16)
    w_rnn_out = p["w_rnn_out"].astype(BF16)
    w_attn_out = p["w_attn_out"].astype(BF16)
    w_mix_out = p["w_mix_out"].astype(BF16)
    wr = jnp.pad(p["w_router"], ((0, 0), (0, LANES - n_exp)))
    wr_hi = wr.astype(BF16)
    wr_lo = (wr - wr_hi.astype(F32)).astype(BF16)
    br = jnp.pad(p["b_router"], (0, LANES - n_exp)).reshape(1, LANES)
    lam = (jnp.exp(jnp.sum(p["lambda_q1"] * p["lambda_k1"]))
           - jnp.exp(jnp.sum(p["lambda_q2"] * p["lambda_k2"])) + lam_init).reshape(1).astype(F32)

    xs_g = xs.reshape(1, db, d)

    u_p = _inproj(xp, mod_p[1], mod_p[0], p["g_pre_mix"], w_in, tm)
    u_s = _inproj(xs_g, mod_s[1], mod_s[0], p["g_pre_mix"], p["w_in"], db)

    cos_p, sin_p = _rope_tables(jnp.arange(s, dtype=jnp.int32))
    cos_s, sin_s = _rope_tables(jnp.full((db,), past, dtype=jnp.int32))
    qb_p, kf_p, kb_p, vf_p, vb_p = _prep(u_p, cos_p, sin_p, tm, q_w, k_w, v_w, q_off, k_off, v_off)
    qb_s, kf_s, kb_s, vf_s, vb_s = _prep(u_s, cos_s, sin_s, db, q_w, k_w, v_w, q_off, k_off, v_off)

    attn_p = _flash(lam, qb_p, kb_p, vb_p, p["g_subln"], lam_init, 256)
    qt = _decode_q(qb_s[0])
    k_new = kf_s.reshape(db, N_KV_HEADS, 2 * HEAD_DIM)[:, _stream_kv(), :]
    v_new = vf_s.reshape(db, N_KV_HEADS, V_DIM)[:, _stream_kv(), :]
    attn_s = _decode(page_table, lam, qt, k_new, v_new, p["g_subln"], cache_k, cache_v,
                     lam_init, min(32, page_table.shape[1])).reshape(1, db, N_HEADS * V_DIM)

    nw = p["conv_w"].shape[0]
    zero_conv = jnp.zeros((b, 8, d_rnn), F32)
    zero_h = jnp.zeros((b, 1, d_rnn), F32)
    rnn_p, conv_p8, h_p = _lru_scan(u_p, zero_conv, zero_h, p["conv_w"], p["conv_b"], wa,
                                    p["b_gate_a"], wx, p["b_gate_x"], p["lru_lambda"], w_rnn_out,
                                    256, d_rnn, True)
    conv_p = conv_p8[:, 8 - (nw - 1):, :]
    conv_state_t = jnp.transpose(conv_state, (1, 0, 2))
    rnn_s, h_s = _lru_step(u_s, conv_state_t, h_state, p["conv_w"], p["conv_b"], p["w_gate_a"],
                           p["b_gate_a"], p["w_gate_x"], p["b_gate_x"], p["lru_lambda"],
                           p["w_rnn_out"], d_rnn, past == 0)
    conv_s = jnp.concatenate([conv_state[:, 1:, :], u_s[0][:, None, :d_rnn]], axis=1)

    n_p = b * s
    h2_rows = TOP_K * (n_p + db)
    x1_p, h2_all, ti_p, tg_p = _merge(rnn_p, attn_p, u_p, xp, mod_p[2], mod_p[4], mod_p[3],
                                      w_attn_out, w_mix_out, p["g_post_mix"], p["g_pre_ffn"],
                                      wr_hi, wr_lo, br, n_exp, tm, gr_off // d, ga_off // d,
                                      h2_rows, 0)
    x1_s, h2_all, ti_s, tg_s = _merge(rnn_s.reshape(1, db, d), attn_s, u_s, xs_g, mod_s[2],
                                      mod_s[4], mod_s[3], p["w_attn_out"], p["w_mix_out"],
                                      p["g_post_mix"], p["g_pre_ffn"], wr_hi, wr_lo, br, n_exp, db,
                                      gr_off // d, ga_off // d, h2_rows, n_p, h2_all)

    ti_all = jnp.concatenate([ti_p.reshape(n_p, LANES), ti_s.reshape(db, LANES)], axis=0)[:, :TOP_K]
    te_tm = 512
    dest, row_token, tile_expert, n_active = _route(ti_all, n_exp, te_tm)
    x_sorted = _gather_rows(h2_all, row_token)
    o_sorted = _experts(tile_expert, n_active, x_sorted, p["w_gate_e"],
                        p["b_gate_e"][:, None, :], p["w_up_e"], p["b_up_e"][:, None, :],
                        p["w_down_e"], p["b_down_e"][:, None, :], te_tm)
    outs_p = [_gather_rows(o_sorted, dest[:n_p, k]).reshape(b, s, d) for k in range(TOP_K)]
    outs_s = [_gather_rows(o_sorted, dest[n_p:, k]).reshape(1, db, d) for k in range(TOP_K)]
    y_p = _combine(outs_p, tg_p, x1_p, mod_p[5], p["g_post_ffn"], tm)
    y_s = _combine(outs_s, tg_s, x1_s, mod_s[5], p["g_post_ffn"], db)

    state = (kf_p.reshape(b, s, N_KV_HEADS, 2 * HEAD_DIM), vf_p.reshape(b, s, N_KV_HEADS, V_DIM),
             kf_s.reshape(db, 1, N_KV_HEADS, 2 * HEAD_DIM), vf_s.reshape(db, 1, N_KV_HEADS, V_DIM),
             conv_p, conv_s, h_p.reshape(b, d_rnn), h_s)
    return y_p, y_s.reshape(db, 1, d), state


def kernel(x_prompt, x_sample, cache_k, cache_v, state_conv, state_rglru, page_table, c_prompt, c_sample, w_ada, b_ada, g_pre_mix, g_post_mix, g_pre_ffn, g_post_ffn, w_in, conv_w, conv_b, w_gate_a, b_gate_a, w_gate_x, b_gate_x, lru_lambda, w_rnn_out, lambda_q1, lambda_k1, lambda_q2, lambda_k2, g_subln, w_attn_out, w_mix_out, w_router, b_router, w_gate_e, b_gate_e, w_up_e, b_up_e, w_down_e, b_down_e):
    depth = w_in.shape[0]
    db, dec_seq, d = x_sample.shape
    assert dec_seq == 1
    b = x_prompt.shape[0]
    n_pool, page = cache_k.shape[1], cache_k.shape[2]
    pad_rows = (-(db + b)) % 8
    c_all = jnp.concatenate([c_sample, c_prompt, jnp.zeros((pad_rows, d), F32)], axis=0)
    xp, xs = x_prompt, x_sample
    states = []
    params = dict(w_ada=w_ada, b_ada=b_ada, g_pre_mix=g_pre_mix, g_post_mix=g_post_mix,
                  g_pre_ffn=g_pre_ffn, g_post_ffn=g_post_ffn, w_in=w_in, conv_w=conv_w,
                  conv_b=conv_b, w_gate_a=w_gate_a, b_gate_a=b_gate_a, w_gate_x=w_gate_x,
                  b_gate_x=b_gate_x, lru_lambda=lru_lambda, w_rnn_out=w_rnn_out,
                  lambda_q1=lambda_q1, lambda_k1=lambda_k1, lambda_q2=lambda_q2,
                  lambda_k2=lambda_k2, g_subln=g_subln, w_attn_out=w_attn_out,
                  w_mix_out=w_mix_out, w_router=w_router, b_router=b_router,
                  w_gate_e=w_gate_e, b_gate_e=b_gate_e, w_up_e=w_up_e, b_up_e=b_up_e,
                  w_down_e=w_down_e, b_down_e=b_down_e)
    take = (lambda a, l: a.reshape(a.shape[1:])) if depth == 1 else (lambda a, l: a[l])
    for l in range(depth):
        p = {name: take(a, l) for name, a in params.items()}
        lam_init = 0.8 - 0.6 * math.exp(-0.3 * l)
        ck = take(cache_k, l).reshape(n_pool, page * N_KV_HEADS, -1)
        cv = take(cache_v, l).reshape(n_pool, page * N_KV_HEADS, -1)
        xp, xs_new, st = _layer(xp, xs.reshape(db, d), c_all, ck, cv, take(state_conv, l),
                                take(state_rglru, l), page_table, page_table.shape[1] * page,
                                p, lam_init)
        xs = xs_new
        states.append(st)
    stacked = [jnp.stack([st[i] for st in states]) for i in range(8)]
    return (xp, xs, *stacked)
```

```python
import functools
import math

import jax
import jax.numpy as jnp
from jax import lax
from jax.experimental import pallas as pl
from jax.experimental.pallas import tpu as pltpu

F32 = jnp.float32
BF16 = jnp.bfloat16

N_HEADS = 8
N_KV_HEADS = 4
HEAD_DIM = 64
V_DIM = 2 * HEAD_DIM
LRU_C = 8.0
ROPE_THETA = 10000.0
TOP_K = 4
SWIGLU_ALPHA = 1.702
SWIGLU_LIMIT = 7.0
EPS = 1e-6

LANES = 128
TOKEN_TILE = 512
FLASH_Q_TILE = 512
FLASH_CHAIN = 256
FLASH_KEYS = 2048
SCAN_CHUNK = 256
EXPERT_TILE = 512
DECODE_PAGES = 32
VMEM_LIMIT = 56 * 1024 * 1024
NEG_INF = float("-inf")


def _cparams(*sem):
    return pltpu.CompilerParams(dimension_semantics=sem, vmem_limit_bytes=VMEM_LIMIT)


def _rms(x, g):
    return x * lax.rsqrt(jnp.mean(x * x, axis=-1, keepdims=True) + EPS) * g


def _sigmoid(x):
    return 0.5 * jnp.tanh(0.5 * x) + 0.5


def _split(a):
    hi = a.astype(BF16)
    return hi, (a - hi.astype(F32)).astype(BF16)


def _mm(a, w):
    if w.dtype == BF16:
        return jnp.dot(a.astype(BF16), w, preferred_element_type=F32)
    a_hi, a_lo = _split(a.astype(F32))
    w_hi, w_lo = _split(w)
    return (jnp.dot(a_hi, w_hi, preferred_element_type=F32)
            + jnp.dot(a_lo, w_hi, preferred_element_type=F32)
            + jnp.dot(a_hi, w_lo, preferred_element_type=F32))


def _ada_kernel(c_ref, w_ref, b_ref, o_ref):
    c = c_ref[...]
    s = (c * _sigmoid(c)).astype(BF16)
    o_ref[...] = jnp.dot(s, w_ref[...].astype(BF16), preferred_element_type=F32) + b_ref[...]


def _ada(c, w, b):
    m, d = c.shape
    n = w.shape[1]
    tn = 1024
    return pl.pallas_call(
        _ada_kernel,
        grid=(n // tn,),
        in_specs=[pl.BlockSpec((m, d), lambda j: (0, 0)),
                  pl.BlockSpec((d, tn), lambda j: (0, j)),
                  pl.BlockSpec((1, tn), lambda j: (0, j))],
        out_specs=pl.BlockSpec((m, tn), lambda j: (0, j)),
        out_shape=jax.ShapeDtypeStruct((m, n), F32),
        compiler_params=_cparams("arbitrary"),
        name="ada_mod",
    )(c, w, b.reshape(1, n))


def _inproj_kernel(x_ref, sc_ref, sh_ref, g_ref, w_ref, o_ref):
    h = _rms(x_ref[0], g_ref[...])
    h = h * (1.0 + sc_ref[0]) + sh_ref[0]
    o_ref[0] = _mm(h, w_ref[...])


def _mod_spec(mod, tm, n_lead):
    per_row = mod.shape[1] != 1
    d = mod.shape[2]
    rows = tm if per_row else 1
    if n_lead == 1:
        return pl.BlockSpec((1, rows, d), lambda n, g, i: (g, i if per_row else 0, 0))
    return pl.BlockSpec((1, rows, d), lambda g, i: (g, i if per_row else 0, 0))


def _inproj(x, sc, sh, gain, w_bf, tm):
    g_, t, d = x.shape
    n = w_bf.shape[1]
    tn = 2048
    return pl.pallas_call(
        _inproj_kernel,
        grid=(n // tn, g_, t // tm),
        in_specs=[pl.BlockSpec((1, tm, d), lambda n, g, i: (g, i, 0)),
                  _mod_spec(sc, tm, 1), _mod_spec(sh, tm, 1),
                  pl.BlockSpec((1, d), lambda n, g, i: (0, 0)),
                  pl.BlockSpec((d, tn), lambda n, g, i: (0, n))],
        out_specs=pl.BlockSpec((1, tm, tn), lambda n, g, i: (g, i, n)),
        out_shape=jax.ShapeDtypeStruct((g_, t, n), F32),
        compiler_params=_cparams("arbitrary", "arbitrary", "arbitrary"),
        name="in_proj",
    )(x, sc, sh, gain.reshape(1, d), w_bf)


def _rope_heads(x, c, s, first_half):
    outs = []
    for h in range(x.shape[1] // LANES):
        xh = x[:, h * LANES:(h + 1) * LANES]
        swapped = jnp.where(first_half, pltpu.roll(xh, LANES - HEAD_DIM // 2, 1),
                            pltpu.roll(xh, HEAD_DIM // 2, 1))
        outs.append(xh * c + swapped * s)
    return jnp.concatenate(outs, axis=1)


def _prep_kernel(q_ref, k_ref, v_ref, c_ref, s_ref, qb_ref, kf_ref, kb_ref, vf_ref, vb_ref):
    c = c_ref[...]
    s = s_ref[...]
    lane = lax.broadcasted_iota(jnp.int32, c.shape, 1)
    first_half = (lane & (HEAD_DIM - 1)) < HEAD_DIM // 2
    q = _rope_heads(q_ref[0], c, s, first_half) * (HEAD_DIM ** -0.5)
    k = _rope_heads(k_ref[0], c, s, first_half)
    v = v_ref[0]
    qb_ref[0] = q.astype(BF16)
    kb_ref[0] = k.astype(BF16)
    tm = k.shape[0]
    n_kv = k.shape[1] // LANES
    for j in range(n_kv):
        kf_ref[0, pl.ds(j, tm, stride=n_kv), :] = k[:, j * LANES:(j + 1) * LANES]
        vf_ref[0, pl.ds(j, tm, stride=n_kv), :] = v[:, j * LANES:(j + 1) * LANES]
    vb = v.astype(BF16)
    ones = jnp.ones((v.shape[0], V_DIM), BF16)
    parts = []
    for j in range(v.shape[1] // V_DIM):
        parts += [vb[:, j * V_DIM:(j + 1) * V_DIM], ones]
    vb_ref[0] = jnp.concatenate(parts, axis=1)


def _prep(u, cos_t, sin_t, tm, q_w, k_w, v_w, q_off, k_off, v_off):
    g_, t, _ = u.shape
    assert k_w == v_w and k_w % LANES == 0
    n_kv = k_w // LANES
    return pl.pallas_call(
        _prep_kernel,
        grid=(g_, t // tm),
        in_specs=[pl.BlockSpec((1, tm, q_w), lambda g, i: (g, i, q_off // q_w)),
                  pl.BlockSpec((1, tm, k_w), lambda g, i: (g, i, k_off // k_w)),
                  pl.BlockSpec((1, tm, v_w), lambda g, i: (g, i, v_off // v_w)),
                  pl.BlockSpec((tm, LANES), lambda g, i: (i, 0)),
                  pl.BlockSpec((tm, LANES), lambda g, i: (i, 0))],
        out_specs=[pl.BlockSpec((1, tm, q_w), lambda g, i: (g, i, 0)),
                   pl.BlockSpec((1, tm * n_kv, LANES), lambda g, i: (g, i, 0)),
                   pl.BlockSpec((1, tm, k_w), lambda g, i: (g, i, 0)),
                   pl.BlockSpec((1, tm * n_kv, LANES), lambda g, i: (g, i, 0)),
                   pl.BlockSpec((1, tm, 2 * v_w), lambda g, i: (g, i, 0))],
        out_shape=[jax.ShapeDtypeStruct((g_, t, q_w), BF16),
                   jax.ShapeDtypeStruct((g_, t * n_kv, LANES), F32),
                   jax.ShapeDtypeStruct((g_, t, k_w), BF16),
                   jax.ShapeDtypeStruct((g_, t * n_kv, LANES), F32),
                   jax.ShapeDtypeStruct((g_, t, 2 * v_w), BF16)],
        compiler_params=_cparams("arbitrary", "arbitrary"),
        name="qkv_prep",
    )(u, u, u, cos_t, sin_t)


def _flash_kernel(lam_ref, q_ref, k_ref, v_ref, gs_ref, o_ref, q4_scr, m_scr, acc_scr,
                  *, tq, lam_init):
    i = pl.program_id(2)
    n_rep = N_HEADS // N_KV_HEADS
    q = q_ref[0]
    lane = lax.broadcasted_iota(jnp.int32, (tq, LANES), 1)
    is_map1 = lane < HEAD_DIM
    zero = jnp.zeros((tq, LANES), BF16)
    for g in range(n_rep):
        qg = q[:, g * LANES:(g + 1) * LANES]
        q4_scr[(2 * g) * tq:(2 * g + 1) * tq, :] = jnp.where(is_map1, qg, zero)
        q4_scr[(2 * g + 1) * tq:(2 * g + 2) * tq, :] = jnp.where(is_map1, zero, qg)
    m_scr[...] = jnp.full(m_scr.shape, NEG_INF, F32)
    acc_scr[...] = jnp.zeros(acc_scr.shape, F32)
    rows = 2 * n_rep * tq

    def step(start, size, masked):
        for r0 in range(0, rows, FLASH_CHAIN):
            rs = slice(r0, r0 + FLASH_CHAIN)
            q_off = r0 % tq
            n_keys = q_off + FLASH_CHAIN if masked else size
            k = k_ref[0, pl.ds(start, n_keys), :]
            v = v_ref[0, pl.ds(start, n_keys), :]
            s = lax.dot_general(q4_scr[rs, :], k, (((1,), (1,)), ((), ())),
                                preferred_element_type=F32)
            if masked:
                r = lax.broadcasted_iota(jnp.int32, s.shape, 0) + q_off
                c = lax.broadcasted_iota(jnp.int32, s.shape, 1)
                s = jnp.where(r >= c, s, NEG_INF)
            m_prev = m_scr[rs, :]
            m_new = jnp.maximum(m_prev, jnp.max(s, axis=1, keepdims=True))
            alpha = jnp.exp(m_prev - m_new)
            p = jnp.exp(s - jnp.concatenate([m_new] * (n_keys // LANES), axis=1))
            acc_scr[rs, :] = (jnp.concatenate([alpha, alpha], axis=1) * acc_scr[rs, :]
                              + jnp.dot(p.astype(BF16), v, preferred_element_type=F32))
            m_scr[rs, :] = m_new

    n_below = i * tq

    def body(t, carry):
        step(pl.multiple_of(t * FLASH_KEYS, FLASH_KEYS), FLASH_KEYS, False)
        return carry

    lax.fori_loop(0, n_below // FLASH_KEYS, body, 0)
    size = FLASH_KEYS // 2
    while size >= tq:
        @pl.when((n_below & size) != 0)
        def _(size=size):
            step(pl.multiple_of((n_below // (2 * size)) * (2 * size), size), size, False)
        size //= 2

    step(pl.multiple_of(n_below, tq), tq, True)

    o = acc_scr[:, 0:V_DIM] / acc_scr[:, V_DIM:2 * V_DIM]
    lam = lam_ref[0]
    for g in range(n_rep):
        o1 = o[(2 * g) * tq:(2 * g + 1) * tq, :]
        o2 = o[(2 * g + 1) * tq:(2 * g + 2) * tq, :]
        d = _rms(o1 - lam * o2, gs_ref[...]) * (1.0 - lam_init)
        o_ref[0, :, g * LANES:(g + 1) * LANES] = d.astype(BF16)


def _flash(lam, q_bf, k_bf, v_bf, g_subln, lam_init, tq):
    b, s, _ = q_bf.shape
    n_rep = N_HEADS // N_KV_HEADS
    qw = n_rep * 2 * HEAD_DIM
    rows = 2 * n_rep * tq
    kern = functools.partial(_flash_kernel, tq=tq, lam_init=lam_init)
    return pl.pallas_call(
        kern,
        grid=(b, N_KV_HEADS, s // tq),
        in_specs=[pl.BlockSpec(memory_space=pltpu.SMEM),
                  pl.BlockSpec((1, tq, qw), lambda b_, j, i: (b_, i, j)),
                  pl.BlockSpec((1, s, 2 * HEAD_DIM), lambda b_, j, i: (b_, 0, j)),
                  pl.BlockSpec((1, s, 2 * V_DIM), lambda b_, j, i: (b_, 0, j)),
                  pl.BlockSpec((1, V_DIM), lambda b_, j, i: (0, 0))],
        out_specs=pl.BlockSpec((1, tq, n_rep * V_DIM), lambda b_, j, i: (b_, i, j)),
        out_shape=jax.ShapeDtypeStruct((b, s, N_HEADS * V_DIM), BF16),
        scratch_shapes=[pltpu.VMEM((rows, LANES), BF16),
                        pltpu.VMEM((rows, LANES), F32),
                        pltpu.VMEM((rows, 2 * V_DIM), F32)],
        compiler_params=_cparams("arbitrary", "arbitrary", "arbitrary"),
        name="prompt_attn",
    )(lam, q_bf, k_bf, v_bf, g_subln.reshape(1, V_DIM))


def _decode_kernel(pt_ref, lam_ref, qt_ref, kn_ref, vn_ref, gs_ref, *rest, pps, lam_init):
    k_refs = rest[:pps]
    v_refs = rest[pps:2 * pps]
    o_ref = rest[2 * pps]
    m_scr, l_scr, acc_scr = rest[2 * pps + 1:]
    s_id = pl.program_id(1)
    n_steps = pl.num_programs(1)

    @pl.when(s_id == 0)
    def _():
        m_scr[...] = jnp.full(m_scr.shape, NEG_INF, F32)
        l_scr[...] = jnp.zeros(l_scr.shape, F32)
        acc_scr[...] = jnp.zeros(acc_scr.shape, F32)

    n_rep = N_HEADS // N_KV_HEADS
    n_str = 2 * N_HEADS
    page = k_refs[0].shape[1] // N_KV_HEADS
    row_kv = (lax.broadcasted_iota(jnp.int32, (n_str, LANES), 0) & (N_HEADS - 1)) // n_rep

    def kv_rows(refs, j):
        return jnp.concatenate([r[0, pl.ds(j, page, stride=N_KV_HEADS), :] for r in refs],
                               axis=0).astype(BF16)

    s = None
    for j in range(N_KV_HEADS):
        sj = lax.dot_general(qt_ref[0, j], kv_rows(k_refs, j), (((1,), (1,)), ((), ())),
                             preferred_element_type=F32)
        s = sj if s is None else s + sj
    m_prev = m_scr[...]
    m_new = jnp.maximum(m_prev, jnp.max(s, axis=1, keepdims=True))
    alpha = jnp.exp(m_prev - m_new)
    p = jnp.exp(s - jnp.concatenate([m_new] * (s.shape[1] // LANES), axis=1))
    l_scr[...] = alpha * l_scr[...] + jnp.sum(p, axis=1, keepdims=True)
    pb = p.astype(BF16)
    acc = alpha * acc_scr[...]
    for j in range(N_KV_HEADS):
        oj = jnp.dot(pb, kv_rows(v_refs, j), preferred_element_type=F32)
        acc = acc + jnp.where(row_kv == j, oj, 0.0)
    acc_scr[...] = acc
    m_scr[...] = m_new

    @pl.when(s_id == n_steps - 1)
    def _():
        qt = qt_ref[0, 0].astype(F32)
        for j in range(1, N_KV_HEADS):
            qt = qt + qt_ref[0, j].astype(F32)
        s_self = jnp.sum(qt * kn_ref[0], axis=1, keepdims=True)
        m_last = m_scr[...]
        m_fin = jnp.maximum(m_last, s_self)
        a_fin = jnp.exp(m_last - m_fin)
        p_self = jnp.exp(s_self - m_fin)
        l_fin = a_fin * l_scr[...] + p_self
        o = (a_fin * acc_scr[...] + p_self * vn_ref[0]) / l_fin
        d = o[0:N_HEADS, :] - lam_ref[0] * o[N_HEADS:n_str, :]
        o_ref[0] = _rms(d, gs_ref[...]) * (1.0 - lam_init)


def _decode(page_table, lam, qmat, k_new, v_new, g_subln, cache_k, cache_v, lam_init, pps):
    db, n_pages = page_table.shape
    assert n_pages % pps == 0
    n_pool, page_rows, kw = cache_k.shape
    vw = cache_v.shape[2]
    n_str = 2 * N_HEADS
    pt_flat = page_table.reshape(-1)

    def page_spec(width, i):
        return pl.BlockSpec((1, page_rows, width),
                            lambda b, s, pt: (pt[b * n_pages + s * pps + i], 0, 0))

    kern = functools.partial(_decode_kernel, pps=pps, lam_init=lam_init)
    grid_spec = pltpu.PrefetchScalarGridSpec(
        num_scalar_prefetch=1,
        grid=(db, n_pages // pps),
        in_specs=[pl.BlockSpec(memory_space=pltpu.SMEM),
                  pl.BlockSpec((1, N_KV_HEADS, n_str, kw), lambda b, s, pt: (b, 0, 0, 0)),
                  pl.BlockSpec((1, n_str, kw), lambda b, s, pt: (b, 0, 0)),
                  pl.BlockSpec((1, n_str, vw), lambda b, s, pt: (b, 0, 0)),
                  pl.BlockSpec((1, V_DIM), lambda b, s, pt: (0, 0))]
                 + [page_spec(kw, i) for i in range(pps)]
                 + [page_spec(vw, i) for i in range(pps)],
        out_specs=pl.BlockSpec((1, N_HEADS, V_DIM), lambda b, s, pt: (b, 0, 0)),
        scratch_shapes=[pltpu.VMEM((n_str, LANES), F32),
                        pltpu.VMEM((n_str, LANES), F32),
                        pltpu.VMEM((n_str, vw), F32)],
    )
    return pl.pallas_call(
        kern,
        grid_spec=grid_spec,
        out_shape=jax.ShapeDtypeStruct((db, N_HEADS, V_DIM), F32),
        compiler_params=_cparams("arbitrary", "arbitrary"),
        name="decode_attn",
    )(pt_flat, lam, qmat, k_new, v_new, g_subln.reshape(1, V_DIM),
      *([cache_k] * pps), *([cache_v] * pps))


def _gelu_tanh(x):
    return 0.5 * x * (1.0 + jnp.tanh(math.sqrt(2.0 / math.pi) * (x + 0.044715 * (x * x * x))))


def _softplus(z):
    return jnp.maximum(z, 0.0) + jnp.log1p(jnp.exp(-jnp.abs(z)))


def _block_diag(y, w_ref, b_ref):
    nb = w_ref.shape[0]
    parts = [_mm(y[:, n * LANES:(n + 1) * LANES], w_ref[n]) for n in range(nb)]
    return jnp.concatenate(parts, axis=1) + b_ref[...]


def _lru_gates(y, wa_ref, ba_ref, wx_ref, bx_ref, lam_ref, reset):
    if wa_ref.dtype == BF16:
        y_in = y.astype(BF16)
    else:
        y_in = y
    r = _sigmoid(_block_diag(y_in, wa_ref, ba_ref))
    i = _sigmoid(_block_diag(y_in, wx_ref, bx_ref))
    log_a = -LRU_C * r * _softplus(-lam_ref[...])
    a = jnp.exp(log_a)
    mult = jnp.sqrt(-jnp.tanh(log_a) * (a * a + 1.0))
    if reset is not None:
        a = jnp.where(reset, 0.0, a)
        mult = jnp.where(reset, 1.0, mult)
    return a, mult * i * y


def _scan_kernel(xr_ref, yr_ref, cs_ref, h0_ref, cw_ref, cb_ref, wa_ref, ba_ref, wx_ref, bx_ref,
                 lam_ref, wo_ref, o_ref, cn_ref, hl_ref, xp_scr, a_scr, b_scr, h_scr, hc_scr,
                 *, tc, reset_first):
    c = pl.program_id(1)

    @pl.when(c == 0)
    def _():
        xp_scr[0:8, :] = cs_ref[0]
        hc_scr[...] = jnp.broadcast_to(h0_ref[0], hc_scr.shape)

    xr = xr_ref[0]
    xp_scr[8:8 + tc, :] = xr
    cw = cw_ref[...]
    nw = cw.shape[0]
    y = cb_ref[...] + cw[nw - 1:nw, :] * xr
    for j in range(nw - 1):
        off = 8 - (nw - 1) + j
        y = y + cw[j:j + 1, :] * xp_scr[off:off + tc, :]
    tail = xp_scr[tc:tc + 8, :]
    xp_scr[0:8, :] = tail
    cn_ref[0] = tail

    reset = None
    if reset_first:
        row = lax.broadcasted_iota(jnp.int32, xr.shape, 0)
        reset = (row + c * tc) == 0
    a, bx = _lru_gates(y, wa_ref, ba_ref, wx_ref, bx_ref, lam_ref, reset)
    a_scr[...] = a
    b_scr[...] = bx

    def body(t, h):
        h = a_scr[pl.ds(t, 1), :] * h + b_scr[pl.ds(t, 1), :]
        h_scr[pl.ds(t, 1), :] = h
        return h

    h_last = lax.fori_loop(0, tc, body, hc_scr[0:1, :], unroll=8)
    hc_scr[...] = jnp.broadcast_to(h_last, hc_scr.shape)
    hl_ref[0] = h_last
    gated = h_scr[...] * _gelu_tanh(yr_ref[0])
    o_ref[0] = _mm(gated, wo_ref[...]).astype(o_ref.dtype)


def _lru_scan(u, conv_state8, h0, conv_w, conv_b, wa, ba, wx, bx, lam, wo, tc, d, reset_first):
    g_, t, _ = u.shape
    nb = wa.shape[0]
    kern = functools.partial(_scan_kernel, tc=tc, reset_first=reset_first)
    vec = lambda: pl.BlockSpec((1, d), lambda g, c: (0, 0))
    return pl.pallas_call(
        kern,
        grid=(g_, t // tc),
        in_specs=[pl.BlockSpec((1, tc, d), lambda g, c: (g, c, 0)),
                  pl.BlockSpec((1, tc, d), lambda g, c: (g, c, 1)),
                  pl.BlockSpec((1, 8, d), lambda g, c: (g, 0, 0)),
                  pl.BlockSpec((1, 1, d), lambda g, c: (g, 0, 0)),
                  pl.BlockSpec(conv_w.shape, lambda g, c: (0, 0)),
                  vec(),
                  pl.BlockSpec(wa.shape, lambda g, c: (0, 0, 0)), vec(),
                  pl.BlockSpec(wx.shape, lambda g, c: (0, 0, 0)), vec(),
                  vec(),
                  pl.BlockSpec(wo.shape, lambda g, c: (0, 0))],
        out_specs=[pl.BlockSpec((1, tc, d), lambda g, c: (g, c, 0)),
                   pl.BlockSpec((1, 8, d), lambda g, c: (g, 0, 0)),
                   pl.BlockSpec((1, 1, d), lambda g, c: (g, 0, 0))],
        out_shape=[jax.ShapeDtypeStruct((g_, t, d), BF16),
                   jax.ShapeDtypeStruct((g_, 8, d), F32),
                   jax.ShapeDtypeStruct((g_, 1, d), F32)],
        scratch_shapes=[pltpu.VMEM((tc + 8, d), F32),
                        pltpu.VMEM((tc, d), F32),
                        pltpu.VMEM((tc, d), F32),
                        pltpu.VMEM((tc, d), F32),
                        pltpu.VMEM((8, d), F32)],
        compiler_params=_cparams("arbitrary", "arbitrary"),
        name="lru_scan",
    )(u, u, conv_state8, h0, conv_w, conv_b.reshape(1, d), wa, ba.reshape(1, d), wx,
      bx.reshape(1, d), lam.reshape(1, d), wo)


def _lru_step_kernel(xr_ref, yr_ref, s_ref, h0_ref, cw_ref, cb_ref, wa_ref, ba_ref, wx_ref, bx_ref,
                     lam_ref, wo_ref, o_ref, hn_ref, *, reset):
    xr = xr_ref[0]
    cw = cw_ref[...]
    nw = cw.shape[0]
    y = cb_ref[...] + cw[nw - 1:nw, :] * xr
    for j in range(nw - 1):
        y = y + cw[j:j + 1, :] * s_ref[j]
    a, bx = _lru_gates(y, wa_ref, ba_ref, wx_ref, bx_ref, lam_ref,
                       jnp.full(y.shape, True) if reset else None)
    h = a * h0_ref[...] + bx
    hn_ref[...] = h
    gated = h * _gelu_tanh(yr_ref[0])
    o_ref[...] = _mm(gated, wo_ref[...]).astype(o_ref.dtype)


def _lru_step(u, conv_state_t, h0, conv_w, conv_b, wa, ba, wx, bx, lam, wo, d, reset):
    m = u.shape[1]
    kern = functools.partial(_lru_step_kernel, reset=reset)
    vec = lambda: pl.BlockSpec((1, d), lambda i: (0, 0))
    return pl.pallas_call(
        kern,
        grid=(1,),
        in_specs=[pl.BlockSpec((1, m, d), lambda i: (0, 0, 0)),
                  pl.BlockSpec((1, m, d), lambda i: (0, 0, 1)),
                  pl.BlockSpec(conv_state_t.shape, lambda i: (0, 0, 0)),
                  pl.BlockSpec((m, d), lambda i: (0, 0)),
                  pl.BlockSpec(conv_w.shape, lambda i: (0, 0)),
                  vec(),
                  pl.BlockSpec(wa.shape, lambda i: (0, 0, 0)), vec(),
                  pl.BlockSpec(wx.shape, lambda i: (0, 0, 0)), vec(),
                  vec(),
                  pl.BlockSpec(wo.shape, lambda i: (0, 0))],
        out_specs=[pl.BlockSpec((m, d), lambda i: (0, 0)),
                   pl.BlockSpec((m, d), lambda i: (0, 0))],
        out_shape=[jax.ShapeDtypeStruct((m, d), wo.dtype),
                   jax.ShapeDtypeStruct((m, d), F32)],
        compiler_params=_cparams("arbitrary"),
        name="lru_step",
    )(u, u, conv_state_t, h0, conv_w, conv_b.reshape(1, d), wa, ba.reshape(1, d), wx,
      bx.reshape(1, d), lam.reshape(1, d), wo)


def _merge_kernel(rnn_ref, attn_ref, gr_ref, ga_ref, x_ref, g1_ref, sc2_ref, sh2_ref,
                  wao_ref, wmo_ref, gpm_ref, gpf_ref, wrh_ref, wrl_ref, br_ref, *rest, n_exp):
    x1_ref, h2_ref, ti_ref, tg_ref = rest[-4:]
    attn_p = _mm(attn_ref[0], wao_ref[...])
    merged = (_sigmoid(gr_ref[0]) * rnn_ref[0].astype(F32)
              + _sigmoid(ga_ref[0]) * attn_p)
    mix = _mm(merged, wmo_ref[...])
    x1 = x_ref[0] + g1_ref[0] * _rms(mix, gpm_ref[...])
    x1_ref[0] = x1
    h2 = _rms(x1, gpf_ref[...]) * (1.0 + sc2_ref[0]) + sh2_ref[0]
    h2_hi = h2.astype(BF16)
    h2_ref[...] = h2_hi
    h2_lo = (h2 - h2_hi.astype(F32)).astype(BF16)
    wrh = wrh_ref[...]
    logits = (jnp.dot(h2_hi, wrh, preferred_element_type=F32)
              + jnp.dot(h2_lo, wrh, preferred_element_type=F32)
              + jnp.dot(h2_hi, wrl_ref[...], preferred_element_type=F32)) + br_ref[...]
    lane = lax.broadcasted_iota(jnp.int32, logits.shape, 1)
    lane_f = lane.astype(F32)
    l = jnp.where(lane < n_exp, logits, NEG_INF)
    vals, idxs = [], []
    for _ in range(TOP_K):
        mx = jnp.max(l, axis=1, keepdims=True)
        idx = jnp.min(jnp.where(l == mx, lane_f, float(LANES)), axis=1, keepdims=True)
        vals.append(mx)
        idxs.append(idx)
        l = jnp.where(lane_f == idx, NEG_INF, l)
    es = [jnp.exp(v - vals[0]) for v in vals]
    den = es[0]
    for e in es[1:]:
        den = den + e
    ti = jnp.zeros(logits.shape, F32)
    tg = jnp.zeros(logits.shape, F32)
    for k in range(TOP_K):
        ti = jnp.where(lane == k, idxs[k], ti)
        tg = jnp.where(lane == k, es[k] / den, tg)
    ti_ref[0] = ti.astype(jnp.int32)
    tg_ref[0] = tg


def _merge(rnn, attn, u, x, g1, sc2, sh2, wao, wmo, gpm, gpf, wr_hi, wr_lo, br, n_exp, tm,
           gr_blk, ga_blk, h2_rows, h2_row0, h2_buf=None):
    g_, t, d = x.shape
    nt = t // tm
    assert h2_row0 % tm == 0
    kern = functools.partial(_merge_kernel, n_exp=n_exp)
    tok = lambda: pl.BlockSpec((1, tm, d), lambda g, i: (g, i, 0))
    full2 = lambda a: pl.BlockSpec(a.shape, lambda g, i: (0, 0))
    vec = lambda: pl.BlockSpec((1, d), lambda g, i: (0, 0))
    in_specs = [tok(), tok(),
                pl.BlockSpec((1, tm, d), lambda g, i: (g, i, gr_blk)),
                pl.BlockSpec((1, tm, d), lambda g, i: (g, i, ga_blk)),
                tok(), _mod_spec(g1, tm, 0), _mod_spec(sc2, tm, 0), _mod_spec(sh2, tm, 0),
                full2(wao), full2(wmo), vec(), vec(), full2(wr_hi), full2(wr_lo), full2(br)]
    args = [rnn, attn, u, u, x, g1, sc2, sh2, wao, wmo, gpm.reshape(1, d), gpf.reshape(1, d),
            wr_hi, wr_lo, br]
    aliases = {}
    if h2_buf is not None:
        in_specs.append(pl.BlockSpec(memory_space=pl.ANY))
        args.append(h2_buf)
        aliases = {len(args) - 1: 1}
    return pl.pallas_call(
        kern,
        grid=(g_, nt),
        in_specs=in_specs,
        out_specs=[tok(),
                   pl.BlockSpec((tm, d), lambda g, i: (h2_row0 // tm + g * nt + i, 0)),
                   pl.BlockSpec((1, tm, LANES), lambda g, i: (g, i, 0)),
                   pl.BlockSpec((1, tm, LANES), lambda g, i: (g, i, 0))],
        out_shape=[jax.ShapeDtypeStruct((g_, t, d), F32),
                   jax.ShapeDtypeStruct((h2_rows, d), BF16),
                   jax.ShapeDtypeStruct((g_, t, LANES), jnp.int32),
                   jax.ShapeDtypeStruct((g_, t, LANES), F32)],
        input_output_aliases=aliases,
        compiler_params=_cparams("arbitrary", "arbitrary"),
        name="merge_router",
    )(*args)


def _expert_kernel(te_ref, na_ref, x_ref, wg_ref, bg_ref, wu_ref, bu_ref, wd_ref, bd_ref, o_ref,
                   wg_scr, wu_scr, wd_scr):
    t = pl.program_id(0)
    e = te_ref[t]
    prev = te_ref[jnp.maximum(t - 1, 0)]
    active = t < na_ref[0]

    @pl.when(active & ((t == 0) | (e != prev)))
    def _():
        wg_scr[...] = wg_ref[0].astype(BF16)
        wu_scr[...] = wu_ref[0].astype(BF16)
        wd_scr[...] = wd_ref[0].astype(BF16)

    @pl.when(active)
    def _():
        x = x_ref[...]
        glu = jnp.minimum(jnp.dot(x, wg_scr[...], preferred_element_type=F32) + bg_ref[0],
                          SWIGLU_LIMIT)
        lin = jnp.clip(jnp.dot(x, wu_scr[...], preferred_element_type=F32) + bu_ref[0],
                       -SWIGLU_LIMIT, SWIGLU_LIMIT)
        hid = glu * _sigmoid(SWIGLU_ALPHA * glu) * (lin + 1.0)
        out = jnp.dot(hid.astype(BF16), wd_scr[...], preferred_element_type=F32) + bd_ref[0]
        o_ref[...] = out.astype(BF16)

    @pl.when(jnp.logical_not(active))
    def _():
        o_ref[...] = jnp.zeros(o_ref.shape, o_ref.dtype)


def _experts(tile_expert, n_active, xs, wg, bg, wu, bu, wd, bd, tm):
    n_rows, d = xs.shape
    n_exp, _, dff = wg.shape
    w_spec = lambda a: pl.BlockSpec((1,) + a.shape[1:], lambda t, te, na: (te[t], 0, 0))
    grid_spec = pltpu.PrefetchScalarGridSpec(
        num_scalar_prefetch=2,
        grid=(n_rows // tm,),
        in_specs=[pl.BlockSpec((tm, d), lambda t, te, na: (t, 0)),
                  w_spec(wg), w_spec(bg), w_spec(wu), w_spec(bu), w_spec(wd), w_spec(bd)],
        out_specs=pl.BlockSpec((tm, d), lambda t, te, na: (t, 0)),
        scratch_shapes=[pltpu.VMEM((d, dff), BF16),
                        pltpu.VMEM((d, dff), BF16),
                        pltpu.VMEM((dff, d), BF16)],
    )
    return pl.pallas_call(
        _expert_kernel,
        grid_spec=grid_spec,
        out_shape=jax.ShapeDtypeStruct((n_rows, d), BF16),
        compiler_params=_cparams("arbitrary"),
        name="expert_ffn",
    )(tile_expert, n_active, xs, wg, bg, wu, bu, wd, bd)


def _combine_kernel(*refs):
    o_refs = refs[:TOP_K]
    tg_ref, x1_ref, g2_ref, gain_ref, out_ref = refs[TOP_K:]
    tg = tg_ref[0]
    y = None
    for k in range(TOP_K):
        term = tg[:, k:k + 1] * o_refs[k][0].astype(F32)
        y = term if y is None else y + term
    out_ref[0] = x1_ref[0] + g2_ref[0] * _rms(y, gain_ref[...])


def _combine(outs, tg, x1, g2, gain, tm):
    g_, t, d = x1.shape
    tok = lambda: pl.BlockSpec((1, tm, d), lambda g, i: (g, i, 0))
    return pl.pallas_call(
        _combine_kernel,
        grid=(g_, t // tm),
        in_specs=[tok() for _ in range(TOP_K)]
                 + [pl.BlockSpec((1, tm, LANES), lambda g, i: (g, i, 0)), tok(),
                    _mod_spec(g2, tm, 0), pl.BlockSpec((1, d), lambda g, i: (0, 0))],
        out_specs=tok(),
        out_shape=jax.ShapeDtypeStruct((g_, t, d), F32),
        compiler_params=_cparams("arbitrary", "arbitrary"),
        name="moe_combine",
    )(*outs, tg, x1, g2, gain.reshape(1, d))


def _route(top_i, n_exp, tm):
    n = top_i.shape[0]
    n_pairs = n * TOP_K
    eid = top_i.reshape(-1)
    experts = jnp.arange(n_exp, dtype=jnp.int32)
    onehot = (eid[:, None] == experts[None, :]).astype(jnp.int32)
    csum = jnp.cumsum(onehot, axis=0)
    counts = csum[-1]
    rank = jnp.sum(onehot * csum, axis=1) - 1
    padded = ((counts + tm - 1) // tm) * tm
    pad_end = jnp.cumsum(padded)
    dest = jnp.sum(onehot * (pad_end - padded)[None, :], axis=1) + rank
    n_tiles = -(-n_pairs // tm) + n_exp
    bits = n_pairs.bit_length()
    low = (1 << bits) - 1
    need_end = jnp.cumsum(padded - counts)
    dummy = jnp.arange(n_tiles * tm - n_pairs, dtype=jnp.int32)
    dummy_exp = jnp.sum((need_end[None, :] <= dummy[:, None]).astype(jnp.int32), axis=1)
    keys = jnp.concatenate([(eid << bits) | jnp.arange(n_pairs, dtype=jnp.int32),
                            (dummy_exp << bits) | low])
    pair = jnp.sort(keys) & low
    spread = jnp.arange(n_tiles * tm, dtype=jnp.int32) % n
    row_token = jnp.where(pair == low, spread, pair // TOP_K)
    tile_start = jnp.arange(n_tiles, dtype=jnp.int32) * tm
    tile_expert = jnp.sum((pad_end[None, :] <= tile_start[:, None]).astype(jnp.int32), axis=1)
    n_active = pad_end[-1] // tm
    last_e = jnp.max(jnp.where(counts > 0, experts, 0))
    tile_expert = jnp.where(tile_start < pad_end[-1], tile_expert, last_e)
    return dest.reshape(n, TOP_K), row_token, tile_expert, n_active.reshape(1)


def _rope_tables(pos):
    half = HEAD_DIM // 2
    inv = ROPE_THETA ** (-jnp.arange(half, dtype=F32) / half)
    ang = pos.astype(F32)[:, None] * inv[None, :]
    cos, sin = jnp.cos(ang), jnp.sin(ang)
    reps = LANES // HEAD_DIM
    cos_t = jnp.tile(cos, (1, 2 * reps))
    sin_t = jnp.tile(jnp.concatenate([-sin, sin], axis=1), (1, reps))
    return cos_t, sin_t


def _stream_kv():
    return (jnp.arange(2 * N_HEADS) % N_HEADS) // (N_HEADS // N_KV_HEADS)


def _decode_q(q_bf):
    db = q_bf.shape[0]
    q4 = q_bf.reshape(db, N_HEADS, 2, HEAD_DIM)
    eye_m = jnp.eye(2, dtype=q_bf.dtype)
    t = q4[:, None, :, :, :] * eye_m[None, :, None, :, None]
    t = t.reshape(db, 2 * N_HEADS, 2 * HEAD_DIM)
    sel = (_stream_kv()[None, :] == jnp.arange(N_KV_HEADS)[:, None]).astype(q_bf.dtype)
    return t[:, None, :, :] * sel[None, :, :, None]


def _gather_rows(a, idx):
    return a.at[idx].get(mode="promise_in_bounds")


def _layer(xp, xs, c_all, cache_k, cache_v, conv_state, h_state, page_table, past, p, lam_init):
    b, s, d = xp.shape
    db = xs.shape[0]
    d_rnn = p["conv_w"].shape[1]
    q_w = N_HEADS * 2 * HEAD_DIM
    k_w = N_KV_HEADS * 2 * HEAD_DIM
    v_w = N_KV_HEADS * V_DIM
    q_off = 2 * d_rnn
    k_off = q_off + q_w
    v_off = k_off + k_w
    gr_off = v_off + v_w
    ga_off = gr_off + d
    n_exp = p["w_router"].shape[1]
    tm = TOKEN_TILE

    mod = _ada(c_all, p["w_ada"], p["b_ada"])
    mod_s = [m.reshape(1, db, d) for m in jnp.split(mod[:db], 6, axis=-1)]
    mod_p = [m.reshape(b, 1, d) for m in jnp.split(mod[db:db + b], 6, axis=-1)]

    w_in = p["w_in"].astype(BF16)
    wa = p["w_gate_a"].astype(BF16)
    wx = p["w_gate_x"].astype(BF16)
    w_rnn_out = p["w_rnn_out"].astype(BF16)
    w_attn_out = p["w_attn_out"].astype(BF16)
    w_mix_out = p["w_mix_out"].astype(BF16)
    wr = jnp.pad(p["w_router"], ((0, 0), (0, LANES - n_exp)))
    wr_hi = wr.astype(BF16)
    wr_lo = (wr - wr_hi.astype(F32)).astype(BF16)
    br = jnp.pad(p["b_router"], (0, LANES - n_exp)).reshape(1, LANES)
    lam = (jnp.exp(jnp.sum(p["lambda_q1"] * p["lambda_k1"]))
           - jnp.exp(jnp.sum(p["lambda_q2"] * p["lambda_k2"])) + lam_init).reshape(1).astype(F32)

    xs_g = xs.reshape(1, db, d)

    u_p = _inproj(xp, mod_p[1], mod_p[0], p["g_pre_mix"], w_in, tm)
    u_s = _inproj(xs_g, mod_s[1], mod_s[0], p["g_pre_mix"], p["w_in"], db)

    cos_p, sin_p = _rope_tables(jnp.arange(s, dtype=jnp.int32))
    cos_s, sin_s = _rope_tables(jnp.full((db,), past, dtype=jnp.int32))
    qb_p, kf_p, kb_p, vf_p, vb_p = _prep(u_p, cos_p, sin_p, tm, q_w, k_w, v_w, q_off, k_off, v_off)
    qb_s, kf_s, kb_s, vf_s, vb_s = _prep(u_s, cos_s, sin_s, db, q_w, k_w, v_w, q_off, k_off, v_off)

    attn_p = _flash(lam, qb_p, kb_p, vb_p, p["g_subln"], lam_init, FLASH_Q_TILE)
    qt = _decode_q(qb_s[0])
    k_new = kf_s.reshape(db, N_KV_HEADS, 2 * HEAD_DIM)[:, _stream_kv(), :]
    v_new = vf_s.reshape(db, N_KV_HEADS, V_DIM)[:, _stream_kv(), :]
    attn_s = _decode(page_table, lam, qt, k_new, v_new, p["g_subln"], cache_k, cache_v,
                     lam_init, min(DECODE_PAGES, page_table.shape[1])
                     ).reshape(1, db, N_HEADS * V_DIM)

    nw = p["conv_w"].shape[0]
    zero_conv = jnp.zeros((b, 8, d_rnn), F32)
    zero_h = jnp.zeros((b, 1, d_rnn), F32)
    rnn_p, conv_p8, h_p = _lru_scan(u_p, zero_conv, zero_h, p["conv_w"], p["conv_b"], wa,
                                    p["b_gate_a"], wx, p["b_gate_x"], p["lru_lambda"], w_rnn_out,
                                    SCAN_CHUNK, d_rnn, True)
    conv_p = conv_p8[:, 8 - (nw - 1):, :]
    conv_state_t = jnp.transpose(conv_state, (1, 0, 2))
    rnn_s, h_s = _lru_step(u_s, conv_state_t, h_state, p["conv_w"], p["conv_b"], p["w_gate_a"],
                           p["b_gate_a"], p["w_gate_x"], p["b_gate_x"], p["lru_lambda"],
                           p["w_rnn_out"], d_rnn, past == 0)
    conv_s = jnp.concatenate([conv_state[:, 1:, :], u_s[0][:, None, :d_rnn]], axis=1)

    n_p = b * s
    h2_rows = TOP_K * (n_p + db)
    x1_p, h2_all, ti_p, tg_p = _merge(rnn_p, attn_p, u_p, xp, mod_p[2], mod_p[4], mod_p[3],
                                      w_attn_out, w_mix_out, p["g_post_mix"], p["g_pre_ffn"],
                                      wr_hi, wr_lo, br, n_exp, tm, gr_off // d, ga_off // d,
                                      h2_rows, 0)
    x1_s, h2_all, ti_s, tg_s = _merge(rnn_s.reshape(1, db, d), attn_s, u_s, xs_g, mod_s[2],
                                      mod_s[4], mod_s[3], p["w_attn_out"], p["w_mix_out"],
                                      p["g_post_mix"], p["g_pre_ffn"], wr_hi, wr_lo, br, n_exp, db,
                                      gr_off // d, ga_off // d, h2_rows, n_p, h2_all)

    ti_all = jnp.concatenate([ti_p.reshape(n_p, LANES), ti_s.reshape(db, LANES)], axis=0)[:, :TOP_K]
    te_tm = EXPERT_TILE
    dest, row_token, tile_expert, n_active = _route(ti_all, n_exp, te_tm)
    x_sorted = _gather_rows(h2_all, row_token)
    o_sorted = _experts(tile_expert, n_active, x_sorted, p["w_gate_e"],
                        p["b_gate_e"][:, None, :], p["w_up_e"], p["b_up_e"][:, None, :],
                        p["w_down_e"], p["b_down_e"][:, None, :], te_tm)
    outs_p = [_gather_rows(o_sorted, dest[:n_p, k]).reshape(b, s, d) for k in range(TOP_K)]
    outs_s = [_gather_rows(o_sorted, dest[n_p:, k]).reshape(1, db, d) for k in range(TOP_K)]
    y_p = _combine(outs_p, tg_p, x1_p, mod_p[5], p["g_post_ffn"], tm)
    y_s = _combine(outs_s, tg_s, x1_s, mod_s[5], p["g_post_ffn"], db)

    state = (kf_p.reshape(b, s, N_KV_HEADS, 2 * HEAD_DIM), vf_p.reshape(b, s, N_KV_HEADS, V_DIM),
             kf_s.reshape(db, 1, N_KV_HEADS, 2 * HEAD_DIM), vf_s.reshape(db, 1, N_KV_HEADS, V_DIM),
             conv_p, conv_s, h_p.reshape(b, d_rnn), h_s)
    return y_p, y_s.reshape(db, 1, d), state


def kernel(x_prompt, x_sample, cache_k, cache_v, state_conv, state_rglru, page_table, c_prompt, c_sample, w_ada, b_ada, g_pre_mix, g_post_mix, g_pre_ffn, g_post_ffn, w_in, conv_w, conv_b, w_gate_a, b_gate_a, w_gate_x, b_gate_x, lru_lambda, w_rnn_out, lambda_q1, lambda_k1, lambda_q2, lambda_k2, g_subln, w_attn_out, w_mix_out, w_router, b_router, w_gate_e, b_gate_e, w_up_e, b_up_e, w_down_e, b_down_e):
    depth = w_in.shape[0]
    db, dec_seq, d = x_sample.shape
    assert dec_seq == 1
    b = x_prompt.shape[0]
    n_pool, page = cache_k.shape[1], cache_k.shape[2]
    pad_rows = (-(db + b)) % 8
    c_all = jnp.concatenate([c_sample, c_prompt, jnp.zeros((pad_rows, d), F32)], axis=0)
    xp, xs = x_prompt, x_sample
    states = []
    params = dict(w_ada=w_ada, b_ada=b_ada, g_pre_mix=g_pre_mix, g_post_mix=g_post_mix,
                  g_pre_ffn=g_pre_ffn, g_post_ffn=g_post_ffn, w_in=w_in, conv_w=conv_w,
                  conv_b=conv_b, w_gate_a=w_gate_a, b_gate_a=b_gate_a, w_gate_x=w_gate_x,
                  b_gate_x=b_gate_x, lru_lambda=lru_lambda, w_rnn_out=w_rnn_out,
                  lambda_q1=lambda_q1, lambda_k1=lambda_k1, lambda_q2=lambda_q2,
                  lambda_k2=lambda_k2, g_subln=g_subln, w_attn_out=w_attn_out,
                  w_mix_out=w_mix_out, w_router=w_router, b_router=b_router,
                  w_gate_e=w_gate_e, b_gate_e=b_gate_e, w_up_e=w_up_e, b_up_e=b_up_e,
                  w_down_e=w_down_e, b_down_e=b_down_e)
    take = (lambda a, l: a.reshape(a.shape[1:])) if depth == 1 else (lambda a, l: a[l])
    for l in range(depth):
        p = {name: take(a, l) for name, a in params.items()}
        lam_init = 0.8 - 0.6 * math.exp(-0.3 * l)
        ck = take(cache_k, l).reshape(n_pool, page * N_KV_HEADS, -1)
        cv = take(cache_v, l).reshape(n_pool, page * N_KV_HEADS, -1)
        xp, xs_new, st = _layer(xp, xs.reshape(db, d), c_all, ck, cv, take(state_conv, l),
                                take(state_rglru, l), page_table, page_table.shape[1] * page,
                                p, lam_init)
        xs = xs_new
        states.append(st)
    stacked = [jnp.stack([st[i] for st in states]) for i in range(8)]
    return (xp, xs, *stacked)
```
